```python
import jax
import jax.numpy as jnp
from jax import lax
import numpy as np

D_MODEL = 2048
BATCH = 2
SEQ = 4096
DEPTH = 2
DEC_BATCH = 4
DEC_SEQ = 8192
PAST_LEN = 128

HEAD_DIM = 128
N_Q_HEADS = 16
N_KV_HEADS = 4
Q_PER_KV = N_Q_HEADS // N_KV_HEADS
WINDOW = 128
ATTN_BLOCK = 128
ROPE_THETA = 10000.0
SGU_WIDTH = D_MODEL
SGU_CHUNK = 128
SGU_GROUP_DIM = 128
SGU_GROUPS = SGU_WIDTH // SGU_GROUP_DIM
D_FF = 5632
Q_WIDTH = N_Q_HEADS * HEAD_DIM
KV_WIDTH = N_KV_HEADS * HEAD_DIM
OFF_Q = 0
OFF_K = OFF_Q + Q_WIDTH
OFF_V = OFF_K + KV_WIDTH
OFF_U = OFF_V + KV_WIDTH
OFF_SV = OFF_U + SGU_WIDTH
OFF_GA = OFF_SV + SGU_WIDTH
OFF_GB = OFF_GA + D_MODEL
IN_WIDTH = OFF_GB + D_MODEL
N_SUBLAYERS = 3
N_MOD = 3 * N_SUBLAYERS
DEEPNORM_ALPHA = (2 * DEPTH) ** 0.25
DEEPNORM_BETA = (8 * DEPTH) ** -0.25
MACARON_WEIGHT = 0.5
LN_EPS = 1e-5

kernel_name = 'hybrid_bidir_swa_sgu_macaron_encoder'


def layer_norm(x, g, b):
    xf = x.astype(jnp.float32)
    mu = jnp.mean(xf, axis=-1, keepdims=True)
    xc = xf - mu
    var = jnp.mean(xc * xc, axis=-1, keepdims=True)
    return (xc * lax.rsqrt(var + LN_EPS) * g + b).astype(x.dtype)


def rope(x):
    s = x.shape[1]
    half = HEAD_DIM // 2
    inv_freq = 1.0 / (ROPE_THETA ** (jnp.arange(half, dtype=jnp.float32) / half))
    ang = jnp.arange(s, dtype=jnp.float32)[:, None] * inv_freq[None, :]
    cos = jnp.cos(ang)[None, :, None, :]
    sin = jnp.sin(ang)[None, :, None, :]
    xf = x.astype(jnp.float32)
    x1, x2 = xf[..., :half], xf[..., half:]
    return jnp.concatenate([x1 * cos - x2 * sin, x2 * cos + x1 * sin], axis=-1).astype(x.dtype)


def band_blocks(t, nb):
    bsz = t.shape[0]
    tp = jnp.pad(t, ((0, 0), (ATTN_BLOCK, ATTN_BLOCK), (0, 0), (0, 0)))
    tp = tp.reshape(bsz, nb + 2, ATTN_BLOCK, t.shape[2], t.shape[3])
    return jnp.concatenate([tp[:, :-2], tp[:, 1:-1], tp[:, 2:]], axis=2)


def windowed_gqa_with_sink(q, k, v, sink):
    bsz, s = q.shape[0], q.shape[1]
    nb = s // ATTN_BLOCK
    qb = q.astype(jnp.float32).reshape(bsz, nb, ATTN_BLOCK, N_KV_HEADS, Q_PER_KV, HEAD_DIM)
    kb = band_blocks(k.astype(jnp.float32), nb)
    vb = band_blocks(v.astype(jnp.float32), nb)
    scores = jnp.einsum('bnqgrd,bnkgd->bngrqk', qb, kb) * (HEAD_DIM ** -0.5)
    qi = jnp.arange(ATTN_BLOCK)[:, None]
    kj = jnp.arange(3 * ATTN_BLOCK)[None, :]
    band = jnp.abs(kj - ATTN_BLOCK - qi) <= WINDOW
    kpos = jnp.arange(nb)[:, None] * ATTN_BLOCK - ATTN_BLOCK + kj
    kvalid = (kpos >= 0) & (kpos < s)
    mask = band[None] & kvalid[:, None, :]
    scores = jnp.where(mask[None, :, None, None], scores, -jnp.inf)
    sink_l = sink.astype(jnp.float32).reshape(N_KV_HEADS, Q_PER_KV)[None, None, :, :, None, None]
    m = jnp.maximum(jnp.max(scores, axis=-1, keepdims=True), sink_l)
    p = jnp.exp(scores - m)
    denom = jnp.sum(p, axis=-1, keepdims=True) + jnp.exp(sink_l - m)
    out = jnp.einsum('bngrqk,bnkgd->bnqgrd', p / denom, vb)
    return out.reshape(bsz, s, Q_WIDTH).astype(v.dtype)


def spatial_gating(u, sv, ln_g, ln_b, w_s, b_s):
    bsz, s = u.shape[0], u.shape[1]
    nc = s // SGU_CHUNK
    vn = layer_norm(sv, ln_g, ln_b).reshape(bsz, nc, SGU_CHUNK, SGU_GROUPS, SGU_GROUP_DIM)
    z = jnp.einsum('gpq,bcqgd->bcpgd', w_s, vn) + b_s.T[None, None, :, :, None]
    return u * z.reshape(bsz, s, SGU_WIDTH)


def swiglu(h, w_in, w_out):
    gate, up = jnp.split(h @ w_in, 2, axis=-1)
    return (jax.nn.silu(gate) * up) @ w_out


def token_mixer(h, w_mix_in, attn_sink, sgu_ln_g, sgu_ln_b, sgu_w, sgu_b, w_br_attn, w_br_sgu, w_mix_out):
    bsz, s = h.shape[0], h.shape[1]
    z = h @ w_mix_in
    q = rope(z[..., OFF_Q:OFF_K].reshape(bsz, s, N_Q_HEADS, HEAD_DIM))
    k = rope(z[..., OFF_K:OFF_V].reshape(bsz, s, N_KV_HEADS, HEAD_DIM))
    v = z[..., OFF_V:OFF_U].reshape(bsz, s, N_KV_HEADS, HEAD_DIM)
    u = jax.nn.gelu(z[..., OFF_U:OFF_SV])
    sv = jax.nn.gelu(z[..., OFF_SV:OFF_GA])
    attn = windowed_gqa_with_sink(q, k, v, attn_sink) @ w_br_attn
    sgu = spatial_gating(u, sv, sgu_ln_g, sgu_ln_b, sgu_w, sgu_b) @ w_br_sgu
    merged = jax.nn.sigmoid(z[..., OFF_GA:OFF_GB]) * attn + jax.nn.sigmoid(z[..., OFF_GB:]) * sgu
    return merged @ w_mix_out


def encoder_layer(x, c, w_ada, b_ada, ln_g, ln_b, ffn1_w_in, ffn1_w_out, w_mix_in, attn_sink,
                  sgu_ln_g, sgu_ln_b, sgu_w, sgu_b, w_br_attn, w_br_sgu, w_mix_out, ffn2_w_in, ffn2_w_out):
    mod = (jax.nn.silu(c) @ w_ada + b_ada).reshape(c.shape[0], N_MOD, 1, D_MODEL)

    def sublayer(x, i, fn, res_w):
        shift, scale, gate = mod[:, 3 * i], mod[:, 3 * i + 1], mod[:, 3 * i + 2]
        y = fn(x * (1.0 + scale) + shift)
        return layer_norm(DEEPNORM_ALPHA * x + res_w * gate * y, ln_g[i], ln_b[i])

    x = sublayer(x, 0, lambda h: swiglu(h, ffn1_w_in, ffn1_w_out), MACARON_WEIGHT)
    x = sublayer(x, 1, lambda h: token_mixer(h, w_mix_in, attn_sink, sgu_ln_g, sgu_ln_b, sgu_w, sgu_b,
                                             w_br_attn, w_br_sgu, w_mix_out), 1.0)
    x = sublayer(x, 2, lambda h: swiglu(h, ffn2_w_in, ffn2_w_out), MACARON_WEIGHT)
    return x


def setup_inputs(seed: int = 0) -> dict:
    key = jax.random.key(seed)
    ks = jax.random.split(key, 24)
    L, D = DEPTH, D_MODEL

    def nrm(k, shape, scale):
        return jax.random.normal(k, shape, jnp.float32) * scale

    w_mix_in = nrm(ks[8], (L, D, IN_WIDTH), D ** -0.5)
    w_mix_in = w_mix_in.at[:, :, OFF_V:OFF_U].multiply(DEEPNORM_BETA)
    return {
        'x_prompt': nrm(ks[0], (BATCH, SEQ, D), 1.0),
        'x_sample': nrm(ks[1], (DEC_BATCH, DEC_SEQ, D), 1.0),
        'c_prompt': nrm(ks[2], (BATCH, D), 1.0),
        'c_sample': nrm(ks[3], (DEC_BATCH, D), 1.0),
        'w_ada': nrm(ks[4], (L, D, N_MOD * D), D ** -0.5),
        'b_ada': nrm(ks[5], (L, N_MOD * D), 0.02),
        'ln_g': 1.0 + nrm(ks[6], (L, N_SUBLAYERS, D), 0.02),
        'ln_b': nrm(ks[7], (L, N_SUBLAYERS, D), 0.02),
        'ffn1_w_in': nrm(ks[9], (L, D, 2 * D_FF), D ** -0.5),
        'ffn1_w_out': nrm(ks[10], (L, D_FF, D), D_FF ** -0.5 * DEEPNORM_BETA),
        'w_mix_in': w_mix_in,
        'attn_sink': nrm(ks[11], (L, N_Q_HEADS), 0.5),
        'sgu_ln_g': 1.0 + nrm(ks[12], (L, SGU_WIDTH), 0.02),
        'sgu_ln_b': nrm(ks[13], (L, SGU_WIDTH), 0.02),
        'sgu_w': nrm(ks[14], (L, SGU_GROUPS, SGU_CHUNK, SGU_CHUNK), 0.5 * SGU_CHUNK ** -0.5),
        'sgu_b': 1.0 + nrm(ks[15], (L, SGU_GROUPS, SGU_CHUNK), 0.02),
        'w_br_attn': nrm(ks[16], (L, Q_WIDTH, D), Q_WIDTH ** -0.5),
        'w_br_sgu': nrm(ks[17], (L, SGU_WIDTH, D), SGU_WIDTH ** -0.5),
        'w_mix_out': nrm(ks[18], (L, D, D), D ** -0.5 * DEEPNORM_BETA),
        'ffn2_w_in': nrm(ks[19], (L, D, 2 * D_FF), D ** -0.5),
        'ffn2_w_out': nrm(ks[20], (L, D_FF, D), D_FF ** -0.5 * DEEPNORM_BETA),
    }


def reference(x_prompt, x_sample, c_prompt, c_sample, w_ada, b_ada, ln_g, ln_b, ffn1_w_in, ffn1_w_out,
              w_mix_in, attn_sink, sgu_ln_g, sgu_ln_b, sgu_w, sgu_b, w_br_attn, w_br_sgu, w_mix_out,
              ffn2_w_in, ffn2_w_out):
    def run(x, c):
        for l in range(DEPTH):
            x = encoder_layer(x, c, w_ada[l], b_ada[l], ln_g[l], ln_b[l], ffn1_w_in[l], ffn1_w_out[l],
                              w_mix_in[l], attn_sink[l], sgu_ln_g[l], sgu_ln_b[l], sgu_w[l], sgu_b[l],
                              w_br_attn[l], w_br_sgu[l], w_mix_out[l], ffn2_w_in[l], ffn2_w_out[l])
        return x

    y_prompt = run(x_prompt, c_prompt)
    y_sample = run(x_sample, c_sample)
    return (y_prompt, y_sample)
```

```python
import functools

import jax
import jax.numpy as jnp
from jax import lax
from jax.experimental import pallas as pl
from jax.experimental.pallas import tpu as pltpu

D_MODEL = 2048
DEPTH = 2
HEAD_DIM = 128
N_Q_HEADS = 16
N_KV_HEADS = 4
Q_PER_KV = N_Q_HEADS // N_KV_HEADS
WINDOW = 128
ATTN_BLOCK = 128
ROPE_THETA = 10000.0
SGU_WIDTH = D_MODEL
SGU_CHUNK = 128
SGU_GROUPS = 16
D_FF = 5632
Q_WIDTH = N_Q_HEADS * HEAD_DIM
KV_WIDTH = N_KV_HEADS * HEAD_DIM
OFF_V = Q_WIDTH + KV_WIDTH
OFF_U = OFF_V + KV_WIDTH
OFF_GA = OFF_U + 2 * SGU_WIDTH
N_MOD = 9
DEEPNORM_ALPHA = (2 * DEPTH) ** 0.25
MACARON_WEIGHT = 0.5
LN_EPS = 1e-5
ATTN_SCALE = HEAD_DIM ** -0.5

BF16 = jnp.bfloat16
F32 = jnp.float32

LANE = 128
SUBLANE = 8
TOKEN_TILE = 512
FF_TILE = 512
MERGE_TILE = 256
ADA_TILE = 1024
VMEM_LIMIT = 56 * 1024 * 1024


def _dot(a, b):
    return jnp.dot(a, b, preferred_element_type=F32)


def _sigmoid(x):
    return 1.0 / (1.0 + jnp.exp(-x))


def _gelu_tanh(x):
    return 0.5 * x * (1.0 + jnp.tanh(0.7978845608028654 * (x + 0.044715 * (x * x * x))))


def _layer_norm(r, g, b):
    mu = jnp.mean(r, axis=-1, keepdims=True)
    xc = r - mu
    var = jnp.mean(xc * xc, axis=-1, keepdims=True)
    return xc * lax.rsqrt(var + LN_EPS) * g + b


def _modulate(x_ref, mod_ref, sub):
    shift = mod_ref[3 * sub:3 * sub + 1, :]
    scale = mod_ref[3 * sub + 1:3 * sub + 2, :]
    return (x_ref[...] * (1.0 + scale) + shift).astype(BF16)


def _params(sem):
    return pltpu.CompilerParams(dimension_semantics=sem, vmem_limit_bytes=VMEM_LIMIT)


def _ada_kernel(c_ref, w_ref, b_ref, o_ref):
    c = c_ref[...]
    s = (c * _sigmoid(c)).astype(BF16)
    o_ref[...] = _dot(s, w_ref[...].astype(BF16)) + b_ref[...]


def _ada_mod(c_all, w_ada, b_ada):
    rows = c_all.shape[0]
    n_out = N_MOD * D_MODEL
    out = pl.pallas_call(
        _ada_kernel,
        grid=(DEPTH, n_out // ADA_TILE),
        in_specs=[
            pl.BlockSpec((rows, D_MODEL), lambda l, j: (0, 0)),
            pl.BlockSpec((None, D_MODEL, ADA_TILE), lambda l, j: (l, 0, j)),
            pl.BlockSpec((None, 1, ADA_TILE), lambda l, j: (l, 0, j)),
        ],
        out_specs=pl.BlockSpec((None, rows, ADA_TILE), lambda l, j: (l, 0, j)),
        out_shape=jax.ShapeDtypeStruct((DEPTH, rows, n_out), F32),
        compiler_params=_params(("arbitrary", "arbitrary")),
        name="ada_mod",
    )(c_all, w_ada, b_ada.reshape(DEPTH, 1, n_out))
    return out.reshape(DEPTH, rows, N_MOD, D_MODEL)


def _ffn_kernel(x_ref, mod_ref, wg_ref, wu_ref, wo_ref, lng_ref, lnb_ref, o_ref, h_ref, acc_ref, *, sub, nf):
    j = pl.program_id(1)

    @pl.when(j == 0)
    def _():
        h_ref[...] = _modulate(x_ref, mod_ref, sub)
        acc_ref[...] = jnp.zeros_like(acc_ref)

    h = h_ref[...]
    g = _dot(h, wg_ref[...])
    u = _dot(h, wu_ref[...])
    a = (g * _sigmoid(g) * u).astype(BF16)
    acc_ref[...] += _dot(a, wo_ref[...])

    @pl.when(j == nf - 1)
    def _():
        gate = mod_ref[3 * sub + 2:3 * sub + 3, :]
        r = DEEPNORM_ALPHA * x_ref[...] + (MACARON_WEIGHT * gate) * acc_ref[...]
        o_ref[...] = _layer_norm(r, lng_ref[...], lnb_ref[...])


def _ffn_sublayer(x, mod, row0, seq, w_in, w_out, ln_g, ln_b, layer, sub):
    m = x.shape[0]
    tm, tf = TOKEN_TILE, FF_TILE
    nf = D_FF // tf
    return pl.pallas_call(
        functools.partial(_ffn_kernel, sub=sub, nf=nf),
        grid=(m // tm, nf),
        in_specs=[
            pl.BlockSpec((tm, D_MODEL), lambda i, j: (i, 0)),
            pl.BlockSpec((None, None, N_MOD, D_MODEL), lambda i, j: (layer, row0 + (i * tm) // seq, 0, 0)),
            pl.BlockSpec((None, D_MODEL, tf), lambda i, j: (layer, 0, j)),
            pl.BlockSpec((None, D_MODEL, tf), lambda i, j: (layer, 0, j + nf)),
            pl.BlockSpec((None, tf, D_MODEL), lambda i, j: (layer, j, 0)),
            pl.BlockSpec((None, None, 1, D_MODEL), lambda i, j: (layer, sub, 0, 0)),
            pl.BlockSpec((None, None, 1, D_MODEL), lambda i, j: (layer, sub, 0, 0)),
        ],
        out_specs=pl.BlockSpec((tm, D_MODEL), lambda i, j: (i, 0)),
        out_shape=jax.ShapeDtypeStruct((m, D_MODEL), F32),
        scratch_shapes=[pltpu.VMEM((tm, D_MODEL), BF16), pltpu.VMEM((tm, D_MODEL), F32)],
        compiler_params=_params(("parallel", "arbitrary")),
        name=f"ffn_l{layer}_s{sub}_m{m}",
    )(x, mod, w_in, w_in, w_out, ln_g, ln_b)


def _qkv_kernel(x_ref, mod_ref, w_ref, cos_ref, sin_ref, q_ref, k_ref, v_ref):
    h = _modulate(x_ref, mod_ref, 1)
    cos = cos_ref[...]
    sin = sin_ref[...]
    chunk = 4 * HEAD_DIM
    for c in range((Q_WIDTH + 2 * KV_WIDTH) // chunk):
        z = _dot(h, w_ref[:, c * chunk:(c + 1) * chunk])
        if c * chunk >= OFF_V:
            v_ref[...] = z.astype(BF16)
            continue
        for t in range(4):
            zh = z[:, t * HEAD_DIM:(t + 1) * HEAD_DIM]
            r = (zh * cos + pltpu.roll(zh, HEAD_DIM // 2, 1) * sin).astype(BF16)
            if c * chunk < Q_WIDTH:
                q_ref[:, c * chunk + t * HEAD_DIM:c * chunk + (t + 1) * HEAD_DIM] = r
            else:
                k_ref[:, t * HEAD_DIM:(t + 1) * HEAD_DIM] = r


def _qkv_proj(x, mod, row0, seq, w_qkv, cos_t, sin_t, layer):
    m = x.shape[0]
    tm = TOKEN_TILE
    n = Q_WIDTH + 2 * KV_WIDTH
    return pl.pallas_call(
        _qkv_kernel,
        grid=(m // tm,),
        in_specs=[
            pl.BlockSpec((tm, D_MODEL), lambda i: (i, 0)),
            pl.BlockSpec((None, None, N_MOD, D_MODEL), lambda i: (layer, row0 + (i * tm) // seq, 0, 0)),
            pl.BlockSpec((None, D_MODEL, n), lambda i: (layer, 0, 0), pipeline_mode=pl.Buffered(1)),
            pl.BlockSpec((tm, HEAD_DIM), lambda i: (i % (seq // tm), 0)),
            pl.BlockSpec((tm, HEAD_DIM), lambda i: (i % (seq // tm), 0)),
        ],
        out_specs=[
            pl.BlockSpec((tm, Q_WIDTH), lambda i: (i, 0)),
            pl.BlockSpec((tm, KV_WIDTH), lambda i: (i, 0)),
            pl.BlockSpec((tm, KV_WIDTH), lambda i: (i, 0)),
        ],
        out_shape=[
            jax.ShapeDtypeStruct((m, Q_WIDTH), BF16),
            jax.ShapeDtypeStruct((m, KV_WIDTH), BF16),
            jax.ShapeDtypeStruct((m, KV_WIDTH), BF16),
        ],
        compiler_params=_params(("parallel",)),
        name=f"qkv_l{layer}_m{m}",
    )(x, mod, w_qkv, cos_t, sin_t)


def _sgu_kernel(x_ref, mod_ref, w_ref, lng_ref, lnb_ref, ws_ref, bs_ref, o_ref, vn_ref):
    tm = x_ref.shape[0]
    h = _modulate(x_ref, mod_ref, 1)
    sv = _gelu_tanh(_dot(h, w_ref[:, SGU_WIDTH:2 * SGU_WIDTH]))
    vn_ref[...] = _layer_norm(sv, lng_ref[...], lnb_ref[...]).astype(BF16)
    nchunk = tm // SGU_CHUNK
    gpc = 4
    for c in range(SGU_GROUPS // gpc):
        lo = c * gpc * LANE
        u = _gelu_tanh(_dot(h, w_ref[:, lo:lo + gpc * LANE]))
        for t in range(gpc):
            g = c * gpc + t
            col = slice(g * LANE, (g + 1) * LANE)
            rhs = jnp.concatenate([vn_ref[n * SGU_CHUNK:(n + 1) * SGU_CHUNK, col] for n in range(nchunk)], axis=1)
            z = _dot(ws_ref[g], rhs)
            for n in range(nchunk):
                rows = slice(n * SGU_CHUNK, (n + 1) * SGU_CHUNK)
                zz = z[:, n * LANE:(n + 1) * LANE] + bs_ref[g]
                o_ref[rows, col] = (u[rows, t * LANE:(t + 1) * LANE] * zz).astype(BF16)


def _sgu_branch(x, mod, row0, seq, w_usv, sgu_ln_g, sgu_ln_b, sgu_w, sgu_b_bc, layer):
    m = x.shape[0]
    tm = TOKEN_TILE
    return pl.pallas_call(
        _sgu_kernel,
        grid=(m // tm,),
        in_specs=[
            pl.BlockSpec((tm, D_MODEL), lambda i: (i, 0)),
            pl.BlockSpec((None, None, N_MOD, D_MODEL), lambda i: (layer, row0 + (i * tm) // seq, 0, 0)),
            pl.BlockSpec((None, D_MODEL, 2 * SGU_WIDTH), lambda i: (layer, 0, 0), pipeline_mode=pl.Buffered(1)),
            pl.BlockSpec((None, 1, SGU_WIDTH), lambda i: (layer, 0, 0)),
            pl.BlockSpec((None, 1, SGU_WIDTH), lambda i: (layer, 0, 0)),
            pl.BlockSpec((None, SGU_GROUPS, SGU_CHUNK, SGU_CHUNK), lambda i: (layer, 0, 0, 0)),
            pl.BlockSpec((None, SGU_GROUPS, SGU_CHUNK, LANE), lambda i: (layer, 0, 0, 0)),
        ],
        out_specs=pl.BlockSpec((tm, SGU_WIDTH), lambda i: (i, 0)),
        out_shape=jax.ShapeDtypeStruct((m, SGU_WIDTH), BF16),
        scratch_shapes=[pltpu.VMEM((tm, SGU_WIDTH), BF16)],
        compiler_params=_params(("parallel",)),
        name=f"sgu_l{layer}_m{m}",
    )(x, mod, w_usv, sgu_ln_g, sgu_ln_b, sgu_w, sgu_b_bc)


def _attn_kernel(sink_ref, q_ref, kp_ref, kc_ref, kn_ref, vp_ref, vc_ref, vn_ref, o_ref, *, layer, seq):
    tq = q_ref.shape[0]
    pos0 = (pl.program_id(0) * tq) % seq
    kall = jnp.concatenate([kp_ref[...], kc_ref[...], kn_ref[...]], axis=0)
    vall = jnp.concatenate([vp_ref[...], vc_ref[...], vn_ref[...]], axis=0)
    span = 3 * ATTN_BLOCK
    qi = lax.broadcasted_iota(jnp.int32, (ATTN_BLOCK, span), 0)
    kj = lax.broadcasted_iota(jnp.int32, (ATTN_BLOCK, span), 1)
    band = jnp.abs(kj - ATTN_BLOCK - qi) <= WINDOW
    for n in range(tq // ATTN_BLOCK):
        kpos = pos0 + (n - 1) * ATTN_BLOCK + kj
        ok = band & (kpos >= 0) & (kpos < seq)
        bias = jnp.where(ok, 0.0, -jnp.inf)
        bias = jnp.concatenate([bias] * Q_PER_KV, axis=0)
        rows = slice(n * ATTN_BLOCK, (n + 1) * ATTN_BLOCK)
        for g in range(N_KV_HEADS):
            kg = kall[n * ATTN_BLOCK:n * ATTN_BLOCK + span, g * HEAD_DIM:(g + 1) * HEAD_DIM]
            vg = vall[n * ATTN_BLOCK:n * ATTN_BLOCK + span, g * HEAD_DIM:(g + 1) * HEAD_DIM]
            heads = [g * Q_PER_KV + r for r in range(Q_PER_KV)]
            qs = jnp.concatenate([q_ref[rows, hh * HEAD_DIM:(hh + 1) * HEAD_DIM] for hh in heads], axis=0)
            s = lax.dot_general(qs, kg, (((1,), (1,)), ((), ())), preferred_element_type=F32)
            s = s * ATTN_SCALE + bias
            sink = jnp.concatenate(
                [jnp.full((ATTN_BLOCK, 1), sink_ref[layer, hh], F32) for hh in heads], axis=0)
            mx = jnp.maximum(jnp.max(s, axis=1, keepdims=True), sink)
            p = jnp.exp(s - mx)
            den = jnp.sum(p, axis=1, keepdims=True) + jnp.exp(sink - mx)
            o = _dot(p.astype(BF16), vg) / den
            for r, hh in enumerate(heads):
                o_ref[rows, hh * HEAD_DIM:(hh + 1) * HEAD_DIM] = (
                    o[r * ATTN_BLOCK:(r + 1) * ATTN_BLOCK].astype(BF16))


def _attention(q, k, v, attn_sink, seq, layer):
    m = q.shape[0]
    tq = TOKEN_TILE
    bpt = tq // ATTN_BLOCK
    last = m // ATTN_BLOCK - 1
    prev_map = lambda i: (jnp.maximum(i * bpt - 1, 0), 0)
    next_map = lambda i: (jnp.minimum(i * bpt + bpt, last), 0)
    cur_map = lambda i: (i, 0)
    return pl.pallas_call(
        functools.partial(_attn_kernel, layer=layer, seq=seq),
        grid=(m // tq,),
        in_specs=[
            pl.BlockSpec(memory_space=pltpu.SMEM),
            pl.BlockSpec((tq, Q_WIDTH), cur_map),
            pl.BlockSpec((ATTN_BLOCK, KV_WIDTH), prev_map),
            pl.BlockSpec((tq, KV_WIDTH), cur_map),
            pl.BlockSpec((ATTN_BLOCK, KV_WIDTH), next_map),
            pl.BlockSpec((ATTN_BLOCK, KV_WIDTH), prev_map),
            pl.BlockSpec((tq, KV_WIDTH), cur_map),
            pl.BlockSpec((ATTN_BLOCK, KV_WIDTH), next_map),
        ],
        out_specs=pl.BlockSpec((tq, Q_WIDTH), cur_map),
        out_shape=jax.ShapeDtypeStruct((m, Q_WIDTH), BF16),
        compiler_params=_params(("parallel",)),
        name=f"attn_l{layer}_m{m}",
    )(attn_sink, q, k, k, k, v, v, v)


def _merge_kernel(x_ref, mod_ref, ao_ref, so_ref, wga_ref, wgb_ref, wba_ref, wbs_ref, wmo_ref, lng_ref, lnb_ref,
                  o_ref, h_ref, acc_ref, *, nj):
    j = pl.program_id(1)

    @pl.when(j == 0)
    def _():
        h_ref[...] = _modulate(x_ref, mod_ref, 1)
        acc_ref[...] = jnp.zeros_like(acc_ref)

    h = h_ref[...]
    attn = _dot(ao_ref[...], wba_ref[...])
    sgu = _dot(so_ref[...], wbs_ref[...])
    merged = _sigmoid(_dot(h, wga_ref[...])) * attn + _sigmoid(_dot(h, wgb_ref[...])) * sgu
    acc_ref[...] += _dot(merged.astype(BF16), wmo_ref[...])

    @pl.when(j == nj - 1)
    def _():
        gate = mod_ref[5:6, :]
        r = DEEPNORM_ALPHA * x_ref[...] + gate * acc_ref[...]
        o_ref[...] = _layer_norm(r, lng_ref[...], lnb_ref[...])


def _merge_sublayer(x, mod, row0, seq, attn_o, sgu_o, w_gates, w_br_attn, w_br_sgu, w_mix_out, ln_g, ln_b, layer):
    m = x.shape[0]
    tm, tn = TOKEN_TILE, MERGE_TILE
    nj = D_MODEL // tn
    return pl.pallas_call(
        functools.partial(_merge_kernel, nj=nj),
        grid=(m // tm, nj),
        in_specs=[
            pl.BlockSpec((tm, D_MODEL), lambda i, j: (i, 0)),
            pl.BlockSpec((None, None, N_MOD, D_MODEL), lambda i, j: (layer, row0 + (i * tm) // seq, 0, 0)),
            pl.BlockSpec((tm, Q_WIDTH), lambda i, j: (i, 0)),
            pl.BlockSpec((tm, SGU_WIDTH), lambda i, j: (i, 0)),
            pl.BlockSpec((None, D_MODEL, tn), lambda i, j: (layer, 0, j)),
            pl.BlockSpec((None, D_MODEL, tn), lambda i, j: (layer, 0, j + nj)),
            pl.BlockSpec((None, Q_WIDTH, tn), lambda i, j: (layer, 0, j)),
            pl.BlockSpec((None, SGU_WIDTH, tn), lambda i, j: (layer, 0, j)),
            pl.BlockSpec((None, tn, D_MODEL), lambda i, j: (layer, j, 0)),
            pl.BlockSpec((None, None, 1, D_MODEL), lambda i, j: (layer, 1, 0, 0)),
            pl.BlockSpec((None, None, 1, D_MODEL), lambda i, j: (layer, 1, 0, 0)),
        ],
        out_specs=pl.BlockSpec((tm, D_MODEL), lambda i, j: (i, 0)),
        out_shape=jax.ShapeDtypeStruct((m, D_MODEL), F32),
        scratch_shapes=[pltpu.VMEM((tm, D_MODEL), BF16), pltpu.VMEM((tm, D_MODEL), F32)],
        compiler_params=_params(("parallel", "arbitrary")),
        name=f"merge_l{layer}_m{m}",
    )(x, mod, attn_o, sgu_o, w_gates, w_gates, w_br_attn, w_br_sgu, w_mix_out, ln_g, ln_b)


def _rope_tables(seq):
    half = HEAD_DIM // 2
    inv_freq = 1.0 / (ROPE_THETA ** (jnp.arange(half, dtype=F32) / half))
    ang = jnp.arange(seq, dtype=F32)[:, None] * inv_freq[None, :]
    cos = jnp.cos(ang)
    sin = jnp.sin(ang)
    return jnp.concatenate([cos, cos], axis=1), jnp.concatenate([-sin, sin], axis=1)


def kernel(x_prompt, x_sample, c_prompt, c_sample, w_ada, b_ada, ln_g, ln_b, ffn1_w_in, ffn1_w_out, w_mix_in, attn_sink, sgu_ln_g, sgu_ln_b, sgu_w, sgu_b, w_br_attn, w_br_sgu, w_mix_out, ffn2_w_in, ffn2_w_out):
    n_prompt = c_prompt.shape[0]
    c_all = jnp.concatenate([c_prompt, c_sample], axis=0)
    c_all = jnp.pad(c_all, ((0, -c_all.shape[0] % SUBLANE), (0, 0)))
    mod = _ada_mod(c_all, w_ada, b_ada)

    ffn1_in, ffn1_out = ffn1_w_in.astype(BF16), ffn1_w_out.astype(BF16)
    ffn2_in, ffn2_out = ffn2_w_in.astype(BF16), ffn2_w_out.astype(BF16)
    w_qkv = w_mix_in[:, :, :OFF_U].astype(BF16)
    w_usv = w_mix_in[:, :, OFF_U:OFF_GA].astype(BF16)
    w_gates = w_mix_in[:, :, OFF_GA:].astype(BF16)
    w_bra, w_brs, w_mo = w_br_attn.astype(BF16), w_br_sgu.astype(BF16), w_mix_out.astype(BF16)
    sgu_w16 = sgu_w.astype(BF16)
    sgu_b_bc = jnp.broadcast_to(sgu_b[..., None], sgu_b.shape + (LANE,))
    ln_g4 = ln_g.reshape(DEPTH, 3, 1, D_MODEL)
    ln_b4 = ln_b.reshape(DEPTH, 3, 1, D_MODEL)
    sgu_g3 = sgu_ln_g.reshape(DEPTH, 1, SGU_WIDTH)
    sgu_b3 = sgu_ln_b.reshape(DEPTH, 1, SGU_WIDTH)

    def run(x3, row0):
        bsz, seq, _ = x3.shape
        x = x3.reshape(bsz * seq, D_MODEL)
        cos_t, sin_t = _rope_tables(seq)
        for l in range(DEPTH):
            x = _ffn_sublayer(x, mod, row0, seq, ffn1_in, ffn1_out, ln_g4, ln_b4, l, 0)
            q, k, v = _qkv_proj(x, mod, row0, seq, w_qkv, cos_t, sin_t, l)
            sgu_o = _sgu_branch(x, mod, row0, seq, w_usv, sgu_g3, sgu_b3, sgu_w16, sgu_b_bc, l)
            attn_o = _attention(q, k, v, attn_sink, seq, l)
            x = _merge_sublayer(x, mod, row0, seq, attn_o, sgu_o, w_gates, w_bra, w_brs, w_mo, ln_g4, ln_b4, l)
            x = _ffn_sublayer(x, mod, row0, seq, ffn2_in, ffn2_out, ln_g4, ln_b4, l, 2)
        return x.reshape(bsz, seq, D_MODEL)

    return run(x_prompt, 0), run(x_sample, n_prompt)
```

```python
import functools

import jax
import jax.numpy as jnp
from jax import lax
from jax.experimental import pallas as pl
from jax.experimental.pallas import tpu as pltpu

D_MODEL = 2048
DEPTH = 2
HEAD_DIM = 128
N_Q_HEADS = 16
N_KV_HEADS = 4
Q_PER_KV = N_Q_HEADS // N_KV_HEADS
WINDOW = 128
ATTN_BLOCK = 128
ROPE_THETA = 10000.0
SGU_WIDTH = D_MODEL
SGU_CHUNK = 128
SGU_GROUPS = 16
D_FF = 5632
Q_WIDTH = N_Q_HEADS * HEAD_DIM
KV_WIDTH = N_KV_HEADS * HEAD_DIM
OFF_V = Q_WIDTH + KV_WIDTH
OFF_U = OFF_V + KV_WIDTH
OFF_GA = OFF_U + 2 * SGU_WIDTH
N_MOD = 9
DEEPNORM_ALPHA = (2 * DEPTH) ** 0.25
MACARON_WEIGHT = 0.5
LN_EPS = 1e-5
ATTN_SCALE = HEAD_DIM ** -0.5
LOG2E = 1.4426950408889634
ATTN_TAIL = 16

BF16 = jnp.bfloat16
F32 = jnp.float32

LANE = 128
SUBLANE = 8
TOKEN_TILE = 512
FFN_TOKEN_TILE = 1024
FF_TILE = 512
MERGE_TILE = 512
ADA_TILE = 1024
VMEM_LIMIT = 56 * 1024 * 1024


def _dot(a, b):
    return jnp.dot(a, b, preferred_element_type=F32)


def _sigmoid(x):
    return 1.0 / (1.0 + jnp.exp(-x))


def _gelu_tanh(x):
    return 0.5 * x * (1.0 + jnp.tanh(0.7978845608028654 * (x + 0.044715 * (x * x * x))))


def _layer_norm(r, g, b):
    mu = jnp.mean(r, axis=-1, keepdims=True)
    xc = r - mu
    var = jnp.mean(xc * xc, axis=-1, keepdims=True)
    return xc * lax.rsqrt(var + LN_EPS) * g + b


def _modulate(x_ref, mod_ref, sub):
    shift = mod_ref[3 * sub:3 * sub + 1, :]
    scale = mod_ref[3 * sub + 1:3 * sub + 2, :]
    return (x_ref[...] * (1.0 + scale) + shift).astype(BF16)


def _params(sem):
    return pltpu.CompilerParams(dimension_semantics=sem, vmem_limit_bytes=VMEM_LIMIT)


def _ada_kernel(c_ref, w_ref, b_ref, o_ref):
    c = c_ref[...]
    s = (c * _sigmoid(c)).astype(BF16)
    o_ref[...] = _dot(s, w_ref[...].astype(BF16)) + b_ref[...]


def _ada_mod(c_all, w_ada, b_ada):
    rows = c_all.shape[0]
    n_out = N_MOD * D_MODEL
    out = pl.pallas_call(
        _ada_kernel,
        grid=(DEPTH, n_out // ADA_TILE),
        in_specs=[
            pl.BlockSpec((rows, D_MODEL), lambda l, j: (0, 0)),
            pl.BlockSpec((None, D_MODEL, ADA_TILE), lambda l, j: (l, 0, j)),
            pl.BlockSpec((None, 1, ADA_TILE), lambda l, j: (l, 0, j)),
        ],
        out_specs=pl.BlockSpec((None, rows, ADA_TILE), lambda l, j: (l, 0, j)),
        out_shape=jax.ShapeDtypeStruct((DEPTH, rows, n_out), F32),
        compiler_params=_params(("arbitrary", "arbitrary")),
        name="ada_mod",
    )(c_all, w_ada, b_ada.reshape(DEPTH, 1, n_out))
    return out.reshape(DEPTH, rows, N_MOD, D_MODEL)


def _ffn_kernel(x_hbm, mod_ref, wg_ref, wu_ref, wo_ref, lng_ref, lnb_ref, o_ref, x_buf, h_ref, x_sem, *, sub, nf):
    i, j = pl.program_id(0), pl.program_id(1)
    tm = x_buf.shape[0]

    def x_copy(tile):
        return pltpu.make_async_copy(x_hbm.at[pl.ds(tile * tm, tm), :], x_buf, x_sem)

    @pl.when((i == 0) & (j == 0))
    def _():
        x_copy(0).start()

    @pl.when(j == 0)
    def _():
        x_copy(i).wait()
        x = x_buf[...]
        shift = mod_ref[3 * sub:3 * sub + 1, :]
        scale = mod_ref[3 * sub + 1:3 * sub + 2, :]
        h_ref[...] = (x * (1.0 + scale) + shift).astype(BF16)
        o_ref[...] = DEEPNORM_ALPHA * x

    @pl.when((j == 1) & (i + 1 < pl.num_programs(0)))
    def _():
        x_copy(i + 1).start()

    h = h_ref[...]
    g = _dot(h, wg_ref[...])
    u = _dot(h, wu_ref[...])
    a = (g * _sigmoid(g) * u).astype(BF16)
    res_gate = MACARON_WEIGHT * mod_ref[3 * sub + 2:3 * sub + 3, :]
    o_ref[...] += res_gate * _dot(a, wo_ref[...])

    @pl.when(j == nf - 1)
    def _():
        o_ref[...] = _layer_norm(o_ref[...], lng_ref[...], lnb_ref[...])


def _ffn_sublayer(x, mod, row0, seq, w_in, w_out, ln_g, ln_b, layer, sub):
    m = x.shape[0]
    tm, tf = FFN_TOKEN_TILE, FF_TILE
    nf = D_FF // tf
    assert nf >= 2 and m % tm == 0 and seq % tm == 0
    return pl.pallas_call(
        functools.partial(_ffn_kernel, sub=sub, nf=nf),
        grid=(m // tm, nf),
        in_specs=[
            pl.BlockSpec(memory_space=pl.ANY),
            pl.BlockSpec((None, None, N_MOD, D_MODEL), lambda i, j: (layer, row0 + (i * tm) // seq, 0, 0)),
            pl.BlockSpec((None, D_MODEL, tf), lambda i, j: (layer, 0, j)),
            pl.BlockSpec((None, D_MODEL, tf), lambda i, j: (layer, 0, j + nf)),
            pl.BlockSpec((None, tf, D_MODEL), lambda i, j: (layer, j, 0)),
            pl.BlockSpec((None, None, 1, D_MODEL), lambda i, j: (layer, sub, 0, 0)),
            pl.BlockSpec((None, None, 1, D_MODEL), lambda i, j: (layer, sub, 0, 0)),
        ],
        out_specs=pl.BlockSpec((tm, D_MODEL), lambda i, j: (i, 0)),
        out_shape=jax.ShapeDtypeStruct((m, D_MODEL), F32),
        scratch_shapes=[
            pltpu.VMEM((tm, D_MODEL), F32),
            pltpu.VMEM((tm, D_MODEL), BF16),
            pltpu.SemaphoreType.DMA(()),
        ],
        compiler_params=_params(("arbitrary", "arbitrary")),
        name=f"ffn_l{layer}_s{sub}_m{m}",
    )(x, mod, w_in, w_in, w_out, ln_g, ln_b)


def _qkv_kernel(x_ref, mod_ref, w_ref, cos_ref, sin_ref, q_ref, k_ref, v_ref):
    h = _modulate(x_ref, mod_ref, 1)
    cos = cos_ref[...]
    sin = sin_ref[...]
    chunk = 4 * HEAD_DIM
    for c in range((Q_WIDTH + 2 * KV_WIDTH) // chunk):
        z = _dot(h, w_ref[:, c * chunk:(c + 1) * chunk])
        if c * chunk >= OFF_V:
            v_ref[...] = z.astype(BF16)
            continue
        for t in range(4):
            zh = z[:, t * HEAD_DIM:(t + 1) * HEAD_DIM]
            r = (zh * cos + pltpu.roll(zh, HEAD_DIM // 2, 1) * sin).astype(BF16)
            if c * chunk < Q_WIDTH:
                q_ref[:, c * chunk + t * HEAD_DIM:c * chunk + (t + 1) * HEAD_DIM] = r
            else:
                k_ref[:, t * HEAD_DIM:(t + 1) * HEAD_DIM] = r


def _qkv_proj(x, mod, row0, seq, w_qkv, cos_t, sin_t, layer):
    m = x.shape[0]
    tm = TOKEN_TILE
    n = Q_WIDTH + 2 * KV_WIDTH
    return pl.pallas_call(
        _qkv_kernel,
        grid=(m // tm,),
        in_specs=[
            pl.BlockSpec((tm, D_MODEL), lambda i: (i, 0)),
            pl.BlockSpec((None, None, N_MOD, D_MODEL), lambda i: (layer, row0 + (i * tm) // seq, 0, 0)),
            pl.BlockSpec((None, D_MODEL, n), lambda i: (layer, 0, 0), pipeline_mode=pl.Buffered(1)),
            pl.BlockSpec((tm, HEAD_DIM), lambda i: (i % (seq // tm), 0)),
            pl.BlockSpec((tm, HEAD_DIM), lambda i: (i % (seq // tm), 0)),
        ],
        out_specs=[
            pl.BlockSpec((tm, Q_WIDTH), lambda i: (i, 0)),
            pl.BlockSpec((tm, KV_WIDTH), lambda i: (i, 0)),
            pl.BlockSpec((tm, KV_WIDTH), lambda i: (i, 0)),
        ],
        out_shape=[
            jax.ShapeDtypeStruct((m, Q_WIDTH), BF16),
            jax.ShapeDtypeStruct((m, KV_WIDTH), BF16),
            jax.ShapeDtypeStruct((m, KV_WIDTH), BF16),
        ],
        compiler_params=_params(("parallel",)),
        name=f"qkv_l{layer}_m{m}",
    )(x, mod, w_qkv, cos_t, sin_t)


def _sgu_kernel(x_ref, mod_ref, w_ref, lng_ref, lnb_ref, ws_ref, bs_ref, o_ref, vn_ref):
    tm = x_ref.shape[0]
    h = _modulate(x_ref, mod_ref, 1)
    sv = _gelu_tanh(_dot(h, w_ref[:, SGU_WIDTH:2 * SGU_WIDTH]))
    vn_ref[...] = _layer_norm(sv, lng_ref[...], lnb_ref[...]).astype(BF16)
    nchunk = tm // SGU_CHUNK
    gpc = 4
    for c in range(SGU_GROUPS // gpc):
        lo = c * gpc * LANE
        u = _gelu_tanh(_dot(h, w_ref[:, lo:lo + gpc * LANE]))
        for t in range(gpc):
            g = c * gpc + t
            col = slice(g * LANE, (g + 1) * LANE)
            rhs = jnp.concatenate([vn_ref[n * SGU_CHUNK:(n + 1) * SGU_CHUNK, col] for n in range(nchunk)], axis=1)
            z = _dot(ws_ref[g], rhs)
            for n in range(nchunk):
                rows = slice(n * SGU_CHUNK, (n + 1) * SGU_CHUNK)
                zz = z[:, n * LANE:(n + 1) * LANE] + bs_ref[g]
                o_ref[rows, col] = (u[rows, t * LANE:(t + 1) * LANE] * zz).astype(BF16)


def _sgu_branch(x, mod, row0, seq, w_usv, sgu_ln_g, sgu_ln_b, sgu_w, sgu_b_bc, layer):
    m = x.shape[0]
    tm = TOKEN_TILE
    return pl.pallas_call(
        _sgu_kernel,
        grid=(m // tm,),
        in_specs=[
            pl.BlockSpec((tm, D_MODEL), lambda i: (i, 0)),
            pl.BlockSpec((None, None, N_MOD, D_MODEL), lambda i: (layer, row0 + (i * tm) // seq, 0, 0)),
            pl.BlockSpec((None, D_MODEL, 2 * SGU_WIDTH), lambda i: (layer, 0, 0), pipeline_mode=pl.Buffered(1)),
            pl.BlockSpec((None, 1, SGU_WIDTH), lambda i: (layer, 0, 0)),
            pl.BlockSpec((None, 1, SGU_WIDTH), lambda i: (layer, 0, 0)),
            pl.BlockSpec((None, SGU_GROUPS, SGU_CHUNK, SGU_CHUNK), lambda i: (layer, 0, 0, 0)),
            pl.BlockSpec((None, SGU_GROUPS, SGU_CHUNK, LANE), lambda i: (layer, 0, 0, 0)),
        ],
        out_specs=pl.BlockSpec((tm, SGU_WIDTH), lambda i: (i, 0)),
        out_shape=jax.ShapeDtypeStruct((m, SGU_WIDTH), BF16),
        scratch_shapes=[pltpu.VMEM((tm, SGU_WIDTH), BF16)],
        compiler_params=_params(("parallel",)),
        name=f"sgu_l{layer}_m{m}",
    )(x, mod, w_usv, sgu_ln_g, sgu_ln_b, sgu_w, sgu_b_bc)


def _attn_kernel(sink_ref, q_ref, kp_ref, kc_ref, kn_ref, vp_ref, vc_ref, vn_ref, o_ref, *, layer, seq):
    tq = q_ref.shape[0]
    pos0 = (pl.program_id(0) * tq) % seq
    kall = jnp.concatenate([kp_ref[...], kc_ref[...], kn_ref[...]], axis=0)
    vall = jnp.concatenate([vp_ref[...], vc_ref[...], vn_ref[...]], axis=0)
    span = 3 * ATTN_BLOCK
    kj = lax.broadcasted_iota(jnp.int32, (span, ATTN_BLOCK), 0)
    qi = lax.broadcasted_iota(jnp.int32, (span, ATTN_BLOCK), 1)
    band = jnp.abs(kj - ATTN_BLOCK - qi) <= WINDOW
    tail_row = lax.broadcasted_iota(jnp.int32, (ATTN_TAIL, 2 * HEAD_DIM), 0)
    tail_col = lax.broadcasted_iota(jnp.int32, (ATTN_TAIL, 2 * HEAD_DIM), 1)
    v_tail = jnp.where((tail_row < 2) & (tail_col >= HEAD_DIM), 1.0, 0.0).astype(BF16)
    ones_blk = jnp.ones((span, HEAD_DIM), BF16)
    p_row = lax.broadcasted_iota(jnp.int32, (ATTN_TAIL, Q_PER_KV * ATTN_BLOCK), 0)
    for n in range(tq // ATTN_BLOCK):
        kpos = pos0 + (n - 1) * ATTN_BLOCK + kj
        ok = band & (kpos >= 0) & (kpos < seq)
        bias = jnp.where(ok, 0.0, -jnp.inf)
        bias = jnp.concatenate([bias] * Q_PER_KV, axis=1)
        rows = slice(n * ATTN_BLOCK, (n + 1) * ATTN_BLOCK)
        for g in range(N_KV_HEADS):
            kg = kall[n * ATTN_BLOCK:n * ATTN_BLOCK + span, g * HEAD_DIM:(g + 1) * HEAD_DIM]
            vg = vall[n * ATTN_BLOCK:n * ATTN_BLOCK + span, g * HEAD_DIM:(g + 1) * HEAD_DIM]
            v_aug = jnp.concatenate([jnp.concatenate([vg, ones_blk], axis=1), v_tail], axis=0)
            heads = [g * Q_PER_KV + r for r in range(Q_PER_KV)]
            qs = jnp.concatenate([q_ref[rows, hh * HEAD_DIM:(hh + 1) * HEAD_DIM] for hh in heads], axis=0)
            s = lax.dot_general(kg, qs, (((1,), (1,)), ((), ())), preferred_element_type=F32)
            s = s * (ATTN_SCALE * LOG2E) + bias
            sink = jnp.concatenate(
                [jnp.full((1, ATTN_BLOCK), sink_ref[layer, hh] * LOG2E, F32) for hh in heads], axis=1)
            mx = jnp.maximum(jnp.max(s, axis=0, keepdims=True), sink)
            p = jnp.exp2(s - mx).astype(BF16)
            es = jnp.exp2(sink - mx)
            es_hi = es.astype(BF16).astype(F32)
            p_tail = jnp.where(p_row == 0, es_hi, jnp.where(p_row == 1, es - es_hi, 0.0)).astype(BF16)
            p_aug = jnp.concatenate([p, p_tail], axis=0)
            o = lax.dot_general(p_aug, v_aug, (((0,), (0,)), ((), ())), preferred_element_type=F32)
            o = o[:, :HEAD_DIM] / o[:, HEAD_DIM:]
            for r, hh in enumerate(heads):
                o_ref[rows, hh * HEAD_DIM:(hh + 1) * HEAD_DIM] = (
                    o[r * ATTN_BLOCK:(r + 1) * ATTN_BLOCK].astype(BF16))


def _attention(q, k, v, attn_sink, seq, layer):
    m = q.shape[0]
    tq = TOKEN_TILE
    bpt = tq // ATTN_BLOCK
    last = m // ATTN_BLOCK - 1
    prev_map = lambda i: (jnp.maximum(i * bpt - 1, 0), 0)
    next_map = lambda i: (jnp.minimum(i * bpt + bpt, last), 0)
    cur_map = lambda i: (i, 0)
    return pl.pallas_call(
        functools.partial(_attn_kernel, layer=layer, seq=seq),
        grid=(m // tq,),
        in_specs=[
            pl.BlockSpec(memory_space=pltpu.SMEM),
            pl.BlockSpec((tq, Q_WIDTH), cur_map),
            pl.BlockSpec((ATTN_BLOCK, KV_WIDTH), prev_map),
            pl.BlockSpec((tq, KV_WIDTH), cur_map),
            pl.BlockSpec((ATTN_BLOCK, KV_WIDTH), next_map),
            pl.BlockSpec((ATTN_BLOCK, KV_WIDTH), prev_map),
            pl.BlockSpec((tq, KV_WIDTH), cur_map),
            pl.BlockSpec((ATTN_BLOCK, KV_WIDTH), next_map),
        ],
        out_specs=pl.BlockSpec((tq, Q_WIDTH), cur_map),
        out_shape=jax.ShapeDtypeStruct((m, Q_WIDTH), BF16),
        compiler_params=_params(("parallel",)),
        name=f"attn_l{layer}_m{m}",
    )(attn_sink, q, k, k, k, v, v, v)


def _merge_kernel(x_ref, mod_ref, ao_ref, so_ref, wga_ref, wgb_ref, wba_ref, wbs_ref, wmo_ref, lng_ref, lnb_ref,
                  o_ref, h_ref, *, nj):
    j = pl.program_id(1)

    @pl.when(j == 0)
    def _():
        h_ref[...] = _modulate(x_ref, mod_ref, 1)
        o_ref[...] = DEEPNORM_ALPHA * x_ref[...]

    h = h_ref[...]
    attn = _dot(ao_ref[...], wba_ref[...])
    sgu = _dot(so_ref[...], wbs_ref[...])
    merged = _sigmoid(_dot(h, wga_ref[...])) * attn + _sigmoid(_dot(h, wgb_ref[...])) * sgu
    o_ref[...] += mod_ref[5:6, :] * _dot(merged.astype(BF16), wmo_ref[...])

    @pl.when(j == nj - 1)
    def _():
        o_ref[...] = _layer_norm(o_ref[...], lng_ref[...], lnb_ref[...])


def _merge_sublayer(x, mod, row0, seq, attn_o, sgu_o, w_gates, w_br_attn, w_br_sgu, w_mix_out, ln_g, ln_b, layer):
    m = x.shape[0]
    tm, tn = TOKEN_TILE, MERGE_TILE
    nj = D_MODEL // tn
    return pl.pallas_call(
        functools.partial(_merge_kernel, nj=nj),
        grid=(m // tm, nj),
        in_specs=[
            pl.BlockSpec((tm, D_MODEL), lambda i, j: (i, 0)),
            pl.BlockSpec((None, None, N_MOD, D_MODEL), lambda i, j: (layer, row0 + (i * tm) // seq, 0, 0)),
            pl.BlockSpec((tm, Q_WIDTH), lambda i, j: (i, 0)),
            pl.BlockSpec((tm, SGU_WIDTH), lambda i, j: (i, 0)),
            pl.BlockSpec((None, D_MODEL, tn), lambda i, j: (layer, 0, j)),
            pl.BlockSpec((None, D_MODEL, tn), lambda i, j: (layer, 0, j + nj)),
            pl.BlockSpec((None, Q_WIDTH, tn), lambda i, j: (layer, 0, j)),
            pl.BlockSpec((None, SGU_WIDTH, tn), lambda i, j: (layer, 0, j)),
            pl.BlockSpec((None, tn, D_MODEL), lambda i, j: (layer, j, 0)),
            pl.BlockSpec((None, None, 1, D_MODEL), lambda i, j: (layer, 1, 0, 0)),
            pl.BlockSpec((None, None, 1, D_MODEL), lambda i, j: (layer, 1, 0, 0)),
        ],
        out_specs=pl.BlockSpec((tm, D_MODEL), lambda i, j: (i, 0)),
        out_shape=jax.ShapeDtypeStruct((m, D_MODEL), F32),
        scratch_shapes=[pltpu.VMEM((tm, D_MODEL), BF16)],
        compiler_params=_params(("parallel", "arbitrary")),
        name=f"merge_l{layer}_m{m}",
    )(x, mod, attn_o, sgu_o, w_gates, w_gates, w_br_attn, w_br_sgu, w_mix_out, ln_g, ln_b)


def _rope_tables(seq):
    half = HEAD_DIM // 2
    inv_freq = 1.0 / (ROPE_THETA ** (jnp.arange(half, dtype=F32) / half))
    ang = jnp.arange(seq, dtype=F32)[:, None] * inv_freq[None, :]
    cos = jnp.cos(ang)
    sin = jnp.sin(ang)
    return jnp.concatenate([cos, cos], axis=1), jnp.concatenate([-sin, sin], axis=1)


def kernel(x_prompt, x_sample, c_prompt, c_sample, w_ada, b_ada, ln_g, ln_b, ffn1_w_in, ffn1_w_out, w_mix_in, attn_sink, sgu_ln_g, sgu_ln_b, sgu_w, sgu_b, w_br_attn, w_br_sgu, w_mix_out, ffn2_w_in, ffn2_w_out):
    n_prompt = c_prompt.shape[0]
    c_all = jnp.concatenate([c_prompt, c_sample], axis=0)
    c_all = jnp.pad(c_all, ((0, -c_all.shape[0] % SUBLANE), (0, 0)))
    mod = _ada_mod(c_all, w_ada, b_ada)

    ffn1_in, ffn1_out = ffn1_w_in.astype(BF16), ffn1_w_out.astype(BF16)
    ffn2_in, ffn2_out = ffn2_w_in.astype(BF16), ffn2_w_out.astype(BF16)
    w_qkv = w_mix_in[:, :, :OFF_U].astype(BF16)
    w_usv = w_mix_in[:, :, OFF_U:OFF_GA].astype(BF16)
    w_gates = w_mix_in[:, :, OFF_GA:].astype(BF16)
    w_bra, w_brs, w_mo = w_br_attn.astype(BF16), w_br_sgu.astype(BF16), w_mix_out.astype(BF16)
    sgu_w16 = sgu_w.astype(BF16)
    sgu_b_bc = jnp.broadcast_to(sgu_b[..., None], sgu_b.shape + (LANE,))
    ln_g4 = ln_g.reshape(DEPTH, 3, 1, D_MODEL)
    ln_b4 = ln_b.reshape(DEPTH, 3, 1, D_MODEL)
    sgu_g3 = sgu_ln_g.reshape(DEPTH, 1, SGU_WIDTH)
    sgu_b3 = sgu_ln_b.reshape(DEPTH, 1, SGU_WIDTH)

    def run(x3, row0):
        bsz, seq, _ = x3.shape
        x = x3.reshape(bsz * seq, D_MODEL)
        cos_t, sin_t = _rope_tables(seq)
        for l in range(DEPTH):
            x = _ffn_sublayer(x, mod, row0, seq, ffn1_in, ffn1_out, ln_g4, ln_b4, l, 0)
            q, k, v = _qkv_proj(x, mod, row0, seq, w_qkv, cos_t, sin_t, l)
            sgu_o = _sgu_branch(x, mod, row0, seq, w_usv, sgu_g3, sgu_b3, sgu_w16, sgu_b_bc, l)
            attn_o = _attention(q, k, v, attn_sink, seq, l)
            x = _merge_sublayer(x, mod, row0, seq, attn_o, sgu_o, w_gates, w_bra, w_brs, w_mo, ln_g4, ln_b4, l)
            x = _ffn_sublayer(x, mod, row0, seq, ffn2_in, ffn2_out, ln_g4, ln_b4, l, 2)
        return x.reshape(bsz, seq, D_MODEL)

    return run(x_prompt, 0), run(x_sample, n_prompt)
```

```python
import functools

import jax
import jax.numpy as jnp
from jax import lax
from jax.experimental import pallas as pl
from jax.experimental.pallas import tpu as pltpu

D_MODEL = 2048
DEPTH = 2
HEAD_DIM = 128
N_Q_HEADS = 16
N_KV_HEADS = 4
Q_PER_KV = N_Q_HEADS // N_KV_HEADS
WINDOW = 128
ATTN_BLOCK = 128
ROPE_THETA = 10000.0
SGU_WIDTH = D_MODEL
SGU_CHUNK = 128
SGU_GROUPS = 16
D_FF = 5632
Q_WIDTH = N_Q_HEADS * HEAD_DIM
KV_WIDTH = N_KV_HEADS * HEAD_DIM
OFF_V = Q_WIDTH + KV_WIDTH
OFF_U = OFF_V + KV_WIDTH
OFF_GA = OFF_U + 2 * SGU_WIDTH
N_MOD = 9
DEEPNORM_ALPHA = (2 * DEPTH) ** 0.25
MACARON_WEIGHT = 0.5
LN_EPS = 1e-5
ATTN_SCALE = HEAD_DIM ** -0.5
LOG2E = 1.4426950408889634
Q_PRESCALE = ATTN_SCALE * LOG2E
ATTN_TAIL = 16

BF16 = jnp.bfloat16
F32 = jnp.float32

LANE = 128
SUBLANE = 8
TOKEN_TILE = 512
FFN_TOKEN_TILE = 1024
FF_TILE = 512
MERGE_TILE = 512
ADA_TILE = 1024
VMEM_LIMIT = 56 * 1024 * 1024


def _dot(a, b):
    return jnp.dot(a, b, preferred_element_type=F32)


def _sigmoid(x):
    return 1.0 / (1.0 + jnp.exp(-x))


def _gelu_tanh(x):
    return 0.5 * x * (1.0 + jnp.tanh(0.7978845608028654 * (x + 0.044715 * (x * x * x))))


def _layer_norm(r, g, b):
    mu = jnp.mean(r, axis=-1, keepdims=True)
    xc = r - mu
    var = jnp.mean(xc * xc, axis=-1, keepdims=True)
    return xc * lax.rsqrt(var + LN_EPS) * g + b


def _modulate(x_ref, mod_ref, sub):
    shift = mod_ref[3 * sub:3 * sub + 1, :]
    scale = mod_ref[3 * sub + 1:3 * sub + 2, :]
    return (x_ref[...] * (1.0 + scale) + shift).astype(BF16)


def _params(sem):
    return pltpu.CompilerParams(dimension_semantics=sem, vmem_limit_bytes=VMEM_LIMIT)


def _ada_kernel(c_ref, w_ref, b_ref, o_ref):
    c = c_ref[...]
    s = (c * _sigmoid(c)).astype(BF16)
    o_ref[...] = _dot(s, w_ref[...].astype(BF16)) + b_ref[...]


def _ada_mod(c_all, w_ada, b_ada):
    rows = c_all.shape[0]
    n_out = N_MOD * D_MODEL
    out = pl.pallas_call(
        _ada_kernel,
        grid=(DEPTH, n_out // ADA_TILE),
        in_specs=[
            pl.BlockSpec((rows, D_MODEL), lambda l, j: (0, 0)),
            pl.BlockSpec((None, D_MODEL, ADA_TILE), lambda l, j: (l, 0, j)),
            pl.BlockSpec((None, 1, ADA_TILE), lambda l, j: (l, 0, j)),
        ],
        out_specs=pl.BlockSpec((None, rows, ADA_TILE), lambda l, j: (l, 0, j)),
        out_shape=jax.ShapeDtypeStruct((DEPTH, rows, n_out), F32),
        compiler_params=_params(("arbitrary", "arbitrary")),
        name="ada_mod",
    )(c_all, w_ada, b_ada.reshape(DEPTH, 1, n_out))
    return out.reshape(DEPTH, rows, N_MOD, D_MODEL)


def _ffn_kernel(x_hbm, mod_ref, wg_ref, wu_ref, wo_ref, lng_ref, lnb_ref, o_ref, x_buf, h_ref, x_sem, *, sub, nf):
    i, j = pl.program_id(0), pl.program_id(1)
    tm = x_buf.shape[0]

    def x_copy(tile):
        return pltpu.make_async_copy(x_hbm.at[pl.ds(tile * tm, tm), :], x_buf, x_sem)

    def gated_ffn_chunk(h):
        g = _dot(h, wg_ref[...])
        u = _dot(h, wu_ref[...])
        a = (g * _sigmoid(g) * u).astype(BF16)
        res_gate = MACARON_WEIGHT * mod_ref[3 * sub + 2:3 * sub + 3, :]
        return res_gate * _dot(a, wo_ref[...])

    halves = [slice(r * (tm // 2), (r + 1) * (tm // 2)) for r in range(2)]

    @pl.when((i == 0) & (j == 0))
    def _():
        x_copy(0).start()

    @pl.when(j == 0)
    def _():
        x_copy(i).wait()
        shift = mod_ref[3 * sub:3 * sub + 1, :]
        scale = mod_ref[3 * sub + 1:3 * sub + 2, :]
        for rows in halves:
            x = x_buf[rows, :]
            h = (x * (1.0 + scale) + shift).astype(BF16)
            h_ref[rows, :] = h
            o_ref[rows, :] = DEEPNORM_ALPHA * x + gated_ffn_chunk(h)

    @pl.when((j == 1) & (i + 1 < pl.num_programs(0)))
    def _():
        x_copy(i + 1).start()

    @pl.when((j > 0) & (j < nf - 1))
    def _():
        o_ref[...] += gated_ffn_chunk(h_ref[...])

    @pl.when(j == nf - 1)
    def _():
        res = [o_ref[rows, :] + gated_ffn_chunk(h_ref[rows, :]) for rows in halves]
        for rows, r in zip(halves, res):
            o_ref[rows, :] = _layer_norm(r, lng_ref[...], lnb_ref[...])


def _ffn_sublayer(x, mod, row0, seq, w_in, w_out, ln_g, ln_b, layer, sub):
    m = x.shape[0]
    tm, tf = FFN_TOKEN_TILE, FF_TILE
    nf = D_FF // tf
    assert nf >= 2 and m % tm == 0 and seq % tm == 0
    return pl.pallas_call(
        functools.partial(_ffn_kernel, sub=sub, nf=nf),
        grid=(m // tm, nf),
        in_specs=[
            pl.BlockSpec(memory_space=pl.ANY),
            pl.BlockSpec((None, None, N_MOD, D_MODEL), lambda i, j: (layer, row0 + (i * tm) // seq, 0, 0)),
            pl.BlockSpec((None, D_MODEL, tf), lambda i, j: (layer, 0, j)),
            pl.BlockSpec((None, D_MODEL, tf), lambda i, j: (layer, 0, j + nf)),
            pl.BlockSpec((None, tf, D_MODEL), lambda i, j: (layer, j, 0)),
            pl.BlockSpec((None, None, 1, D_MODEL), lambda i, j: (layer, sub, 0, 0)),
            pl.BlockSpec((None, None, 1, D_MODEL), lambda i, j: (layer, sub, 0, 0)),
        ],
        out_specs=pl.BlockSpec((tm, D_MODEL), lambda i, j: (i, 0)),
        out_shape=jax.ShapeDtypeStruct((m, D_MODEL), F32),
        scratch_shapes=[
            pltpu.VMEM((tm, D_MODEL), F32),
            pltpu.VMEM((tm, D_MODEL), BF16),
            pltpu.SemaphoreType.DMA(()),
        ],
        compiler_params=_params(("arbitrary", "arbitrary")),
        name=f"ffn_l{layer}_s{sub}_m{m}",
    )(x, mod, w_in, w_in, w_out, ln_g, ln_b)


def _qkv_kernel(x_ref, mod_ref, w_ref, cos_ref, sin_ref, q_ref, k_ref, v_ref):
    h = _modulate(x_ref, mod_ref, 1)
    cos = cos_ref[...]
    sin = sin_ref[...]
    chunk = 4 * HEAD_DIM
    for c in range((Q_WIDTH + 2 * KV_WIDTH) // chunk):
        z = _dot(h, w_ref[:, c * chunk:(c + 1) * chunk])
        if c * chunk >= OFF_V:
            v_ref[...] = z.astype(BF16)
            continue
        for t in range(4):
            zh = z[:, t * HEAD_DIM:(t + 1) * HEAD_DIM]
            r = zh * cos + pltpu.roll(zh, HEAD_DIM // 2, 1) * sin
            if c * chunk < Q_WIDTH:
                q_ref[:, c * chunk + t * HEAD_DIM:c * chunk + (t + 1) * HEAD_DIM] = (r * Q_PRESCALE).astype(BF16)
            else:
                k_ref[:, t * HEAD_DIM:(t + 1) * HEAD_DIM] = r.astype(BF16)


def _qkv_proj(x, mod, row0, seq, w_qkv, cos_t, sin_t, layer):
    m = x.shape[0]
    tm = TOKEN_TILE
    n = Q_WIDTH + 2 * KV_WIDTH
    return pl.pallas_call(
        _qkv_kernel,
        grid=(m // tm,),
        in_specs=[
            pl.BlockSpec((tm, D_MODEL), lambda i: (i, 0)),
            pl.BlockSpec((None, None, N_MOD, D_MODEL), lambda i: (layer, row0 + (i * tm) // seq, 0, 0)),
            pl.BlockSpec((None, D_MODEL, n), lambda i: (layer, 0, 0), pipeline_mode=pl.Buffered(1)),
            pl.BlockSpec((tm, HEAD_DIM), lambda i: (i % (seq // tm), 0)),
            pl.BlockSpec((tm, HEAD_DIM), lambda i: (i % (seq // tm), 0)),
        ],
        out_specs=[
            pl.BlockSpec((tm, Q_WIDTH), lambda i: (i, 0)),
            pl.BlockSpec((tm, KV_WIDTH), lambda i: (i, 0)),
            pl.BlockSpec((tm, KV_WIDTH), lambda i: (i, 0)),
        ],
        out_shape=[
            jax.ShapeDtypeStruct((m, Q_WIDTH), BF16),
            jax.ShapeDtypeStruct((m, KV_WIDTH), BF16),
            jax.ShapeDtypeStruct((m, KV_WIDTH), BF16),
        ],
        compiler_params=_params(("parallel",)),
        name=f"qkv_l{layer}_m{m}",
    )(x, mod, w_qkv, cos_t, sin_t)


def _sgu_kernel(x_ref, mod_ref, w_ref, lng_ref, lnb_ref, ws_ref, bs_ref, o_ref, vn_ref):
    tm = x_ref.shape[0]
    h = _modulate(x_ref, mod_ref, 1)
    sv = _gelu_tanh(_dot(h, w_ref[:, SGU_WIDTH:2 * SGU_WIDTH]))
    vn_ref[...] = _layer_norm(sv, lng_ref[...], lnb_ref[...]).astype(BF16)
    nchunk = tm // SGU_CHUNK
    gpc = 4
    for c in range(SGU_GROUPS // gpc):
        lo = c * gpc * LANE
        u = _gelu_tanh(_dot(h, w_ref[:, lo:lo + gpc * LANE]))
        for t in range(gpc):
            g = c * gpc + t
            col = slice(g * LANE, (g + 1) * LANE)
            rhs = jnp.concatenate([vn_ref[n * SGU_CHUNK:(n + 1) * SGU_CHUNK, col] for n in range(nchunk)], axis=1)
            z = _dot(ws_ref[g], rhs)
            for n in range(nchunk):
                rows = slice(n * SGU_CHUNK, (n + 1) * SGU_CHUNK)
                zz = z[:, n * LANE:(n + 1) * LANE] + bs_ref[g]
                o_ref[rows, col] = (u[rows, t * LANE:(t + 1) * LANE] * zz).astype(BF16)


def _sgu_branch(x, mod, row0, seq, w_usv, sgu_ln_g, sgu_ln_b, sgu_w, sgu_b_bc, layer):
    m = x.shape[0]
    tm = TOKEN_TILE
    return pl.pallas_call(
        _sgu_kernel,
        grid=(m // tm,),
        in_specs=[
            pl.BlockSpec((tm, D_MODEL), lambda i: (i, 0)),
            pl.BlockSpec((None, None, N_MOD, D_MODEL), lambda i: (layer, row0 + (i * tm) // seq, 0, 0)),
            pl.BlockSpec((None, D_MODEL, 2 * SGU_WIDTH), lambda i: (layer, 0, 0), pipeline_mode=pl.Buffered(1)),
            pl.BlockSpec((None, 1, SGU_WIDTH), lambda i: (layer, 0, 0)),
            pl.BlockSpec((None, 1, SGU_WIDTH), lambda i: (layer, 0, 0)),
            pl.BlockSpec((None, SGU_GROUPS, SGU_CHUNK, SGU_CHUNK), lambda i: (layer, 0, 0, 0)),
            pl.BlockSpec((None, SGU_GROUPS, SGU_CHUNK, LANE), lambda i: (layer, 0, 0, 0)),
        ],
        out_specs=pl.BlockSpec((tm, SGU_WIDTH), lambda i: (i, 0)),
        out_shape=jax.ShapeDtypeStruct((m, SGU_WIDTH), BF16),
        scratch_shapes=[pltpu.VMEM((tm, SGU_WIDTH), BF16)],
        compiler_params=_params(("parallel",)),
        name=f"sgu_l{layer}_m{m}",
    )(x, mod, w_usv, sgu_ln_g, sgu_ln_b, sgu_w, sgu_b_bc)


def _attn_kernel(sink_ref, q_ref, kp_ref, kc_ref, kn_ref, vp_ref, vc_ref, vn_ref, o_ref, *, layer, seq):
    tq = q_ref.shape[0]
    pos0 = (pl.program_id(0) * tq) % seq
    kall = jnp.concatenate([kp_ref[...], kc_ref[...], kn_ref[...]], axis=0)
    vall = jnp.concatenate([vp_ref[...], vc_ref[...], vn_ref[...]], axis=0)
    span = 3 * ATTN_BLOCK
    kj = lax.broadcasted_iota(jnp.int32, (span, ATTN_BLOCK), 0)
    qi = lax.broadcasted_iota(jnp.int32, (span, ATTN_BLOCK), 1)
    band = jnp.abs(kj - ATTN_BLOCK - qi) <= WINDOW
    tail_row = lax.broadcasted_iota(jnp.int32, (ATTN_TAIL, 2 * HEAD_DIM), 0)
    tail_col = lax.broadcasted_iota(jnp.int32, (ATTN_TAIL, 2 * HEAD_DIM), 1)
    v_tail = jnp.where((tail_row < 2) & (tail_col >= HEAD_DIM), 1.0, 0.0).astype(BF16)
    ones_blk = jnp.ones((span, HEAD_DIM), BF16)
    p_row = lax.broadcasted_iota(jnp.int32, (ATTN_TAIL, Q_PER_KV * ATTN_BLOCK), 0)
    for n in range(tq // ATTN_BLOCK):
        kpos = pos0 + (n - 1) * ATTN_BLOCK + kj
        ok = band & (kpos >= 0) & (kpos < seq)
        bias = jnp.where(ok, 0.0, -jnp.inf)
        bias = jnp.concatenate([bias] * Q_PER_KV, axis=1)
        bias_prev, bias_next = bias[:ATTN_BLOCK], bias[2 * ATTN_BLOCK:]
        rows = slice(n * ATTN_BLOCK, (n + 1) * ATTN_BLOCK)
        for g in range(N_KV_HEADS):
            kg = kall[n * ATTN_BLOCK:n * ATTN_BLOCK + span, g * HEAD_DIM:(g + 1) * HEAD_DIM]
            vg = vall[n * ATTN_BLOCK:n * ATTN_BLOCK + span, g * HEAD_DIM:(g + 1) * HEAD_DIM]
            v_aug = jnp.concatenate([jnp.concatenate([vg, ones_blk], axis=1), v_tail], axis=0)
            heads = [g * Q_PER_KV + r for r in range(Q_PER_KV)]
            qs = jnp.concatenate([q_ref[rows, hh * HEAD_DIM:(hh + 1) * HEAD_DIM] for hh in heads], axis=0)
            s = lax.dot_general(kg, qs, (((1,), (1,)), ((), ())), preferred_element_type=F32)
            blocks = [s[:ATTN_BLOCK] + bias_prev, s[ATTN_BLOCK:2 * ATTN_BLOCK], s[2 * ATTN_BLOCK:] + bias_next]
            sink = jnp.concatenate(
                [jnp.full((1, ATTN_BLOCK), sink_ref[layer, hh] * LOG2E, F32) for hh in heads], axis=1)
            mx = sink
            for sb in blocks:
                mx = jnp.maximum(mx, jnp.max(sb, axis=0, keepdims=True))
            es = jnp.exp2(sink - mx)
            es_hi = es.astype(BF16).astype(F32)
            p_tail = jnp.where(p_row == 0, es_hi, jnp.where(p_row == 1, es - es_hi, 0.0)).astype(BF16)
            p_aug = jnp.concatenate([jnp.exp2(sb - mx).astype(BF16) for sb in blocks] + [p_tail], axis=0)
            o = lax.dot_general(p_aug, v_aug, (((0,), (0,)), ((), ())), preferred_element_type=F32)
            o = o[:, :HEAD_DIM] / o[:, HEAD_DIM:]
            for r, hh in enumerate(heads):
                o_ref[rows, hh * HEAD_DIM:(hh + 1) * HEAD_DIM] = (
                    o[r * ATTN_BLOCK:(r + 1) * ATTN_BLOCK].astype(BF16))


def _attention(q, k, v, attn_sink, seq, layer):
    m = q.shape[0]
    tq = TOKEN_TILE
    bpt = tq // ATTN_BLOCK
    last = m // ATTN_BLOCK - 1
    prev_map = lambda i: (jnp.maximum(i * bpt - 1, 0), 0)
    next_map = lambda i: (jnp.minimum(i * bpt + bpt, last), 0)
    cur_map = lambda i: (i, 0)
    return pl.pallas_call(
        functools.partial(_attn_kernel, layer=layer, seq=seq),
        grid=(m // tq,),
        in_specs=[
            pl.BlockSpec(memory_space=pltpu.SMEM),
            pl.BlockSpec((tq, Q_WIDTH), cur_map),
            pl.BlockSpec((ATTN_BLOCK, KV_WIDTH), prev_map),
            pl.BlockSpec((tq, KV_WIDTH), cur_map),
            pl.BlockSpec((ATTN_BLOCK, KV_WIDTH), next_map),
            pl.BlockSpec((ATTN_BLOCK, KV_WIDTH), prev_map),
            pl.BlockSpec((tq, KV_WIDTH), cur_map),
            pl.BlockSpec((ATTN_BLOCK, KV_WIDTH), next_map),
        ],
        out_specs=pl.BlockSpec((tq, Q_WIDTH), cur_map),
        out_shape=jax.ShapeDtypeStruct((m, Q_WIDTH), BF16),
        compiler_params=_params(("parallel",)),
        name=f"attn_l{layer}_m{m}",
    )(attn_sink, q, k, k, k, v, v, v)


def _merge_kernel(x_ref, mod_ref, ao_ref, so_ref, wga_ref, wgb_ref, wba_ref, wbs_ref, wmo_ref, lng_ref, lnb_ref,
                  o_ref, h_ref, *, nj):
    j = pl.program_id(1)

    @pl.when(j == 0)
    def _():
        h_ref[...] = _modulate(x_ref, mod_ref, 1)
        o_ref[...] = DEEPNORM_ALPHA * x_ref[...]

    h = h_ref[...]
    attn = _dot(ao_ref[...], wba_ref[...])
    sgu = _dot(so_ref[...], wbs_ref[...])
    merged = _sigmoid(_dot(h, wga_ref[...])) * attn + _sigmoid(_dot(h, wgb_ref[...])) * sgu
    o_ref[...] += mod_ref[5:6, :] * _dot(merged.astype(BF16), wmo_ref[...])

    @pl.when(j == nj - 1)
    def _():
        o_ref[...] = _layer_norm(o_ref[...], lng_ref[...], lnb_ref[...])


def _merge_sublayer(x, mod, row0, seq, attn_o, sgu_o, w_gates, w_br_attn, w_br_sgu, w_mix_out, ln_g, ln_b, layer):
    m = x.shape[0]
    tm, tn = TOKEN_TILE, MERGE_TILE
    nj = D_MODEL // tn
    return pl.pallas_call(
        functools.partial(_merge_kernel, nj=nj),
        grid=(m // tm, nj),
        in_specs=[
            pl.BlockSpec((tm, D_MODEL), lambda i, j: (i, 0)),
            pl.BlockSpec((None, None, N_MOD, D_MODEL), lambda i, j: (layer, row0 + (i * tm) // seq, 0, 0)),
            pl.BlockSpec((tm, Q_WIDTH), lambda i, j: (i, 0)),
            pl.BlockSpec((tm, SGU_WIDTH), lambda i, j: (i, 0)),
            pl.BlockSpec((None, D_MODEL, tn), lambda i, j: (layer, 0, j)),
            pl.BlockSpec((None, D_MODEL, tn), lambda i, j: (layer, 0, j + nj)),
            pl.BlockSpec((None, Q_WIDTH, tn), lambda i, j: (layer, 0, j)),
            pl.BlockSpec((None, SGU_WIDTH, tn), lambda i, j: (layer, 0, j)),
            pl.BlockSpec((None, tn, D_MODEL), lambda i, j: (layer, j, 0)),
            pl.BlockSpec((None, None, 1, D_MODEL), lambda i, j: (layer, 1, 0, 0)),
            pl.BlockSpec((None, None, 1, D_MODEL), lambda i, j: (layer, 1, 0, 0)),
        ],
        out_specs=pl.BlockSpec((tm, D_MODEL), lambda i, j: (i, 0)),
        out_shape=jax.ShapeDtypeStruct((m, D_MODEL), F32),
        scratch_shapes=[pltpu.VMEM((tm, D_MODEL), BF16)],
        compiler_params=_params(("parallel", "arbitrary")),
        name=f"merge_l{layer}_m{m}",
    )(x, mod, attn_o, sgu_o, w_gates, w_gates, w_br_attn, w_br_sgu, w_mix_out, ln_g, ln_b)


def _rope_tables(seq):
    half = HEAD_DIM // 2
    inv_freq = 1.0 / (ROPE_THETA ** (jnp.arange(half, dtype=F32) / half))
    ang = jnp.arange(seq, dtype=F32)[:, None] * inv_freq[None, :]
    cos = jnp.cos(ang)
    sin = jnp.sin(ang)
    return jnp.concatenate([cos, cos], axis=1), jnp.concatenate([-sin, sin], axis=1)


def kernel(x_prompt, x_sample, c_prompt, c_sample, w_ada, b_ada, ln_g, ln_b, ffn1_w_in, ffn1_w_out, w_mix_in, attn_sink, sgu_ln_g, sgu_ln_b, sgu_w, sgu_b, w_br_attn, w_br_sgu, w_mix_out, ffn2_w_in, ffn2_w_out):
    n_prompt = c_prompt.shape[0]
    c_all = jnp.concatenate([c_prompt, c_sample], axis=0)
    c_all = jnp.pad(c_all, ((0, -c_all.shape[0] % SUBLANE), (0, 0)))
    mod = _ada_mod(c_all, w_ada, b_ada)

    ffn1_in, ffn1_out = ffn1_w_in.astype(BF16), ffn1_w_out.astype(BF16)
    ffn2_in, ffn2_out = ffn2_w_in.astype(BF16), ffn2_w_out.astype(BF16)
    w_qkv = w_mix_in[:, :, :OFF_U].astype(BF16)
    w_usv = w_mix_in[:, :, OFF_U:OFF_GA].astype(BF16)
    w_gates = w_mix_in[:, :, OFF_GA:].astype(BF16)
    w_bra, w_brs, w_mo = w_br_attn.astype(BF16), w_br_sgu.astype(BF16), w_mix_out.astype(BF16)
    sgu_w16 = sgu_w.astype(BF16)
    sgu_b_bc = jnp.broadcast_to(sgu_b[..., None], sgu_b.shape + (LANE,))
    ln_g4 = ln_g.reshape(DEPTH, 3, 1, D_MODEL)
    ln_b4 = ln_b.reshape(DEPTH, 3, 1, D_MODEL)
    sgu_g3 = sgu_ln_g.reshape(DEPTH, 1, SGU_WIDTH)
    sgu_b3 = sgu_ln_b.reshape(DEPTH, 1, SGU_WIDTH)

    def run(x3, row0):
        bsz, seq, _ = x3.shape
        x = x3.reshape(bsz * seq, D_MODEL)
        cos_t, sin_t = _rope_tables(seq)
        for l in range(DEPTH):
            x = _ffn_sublayer(x, mod, row0, seq, ffn1_in, ffn1_out, ln_g4, ln_b4, l, 0)
            q, k, v = _qkv_proj(x, mod, row0, seq, w_qkv, cos_t, sin_t, l)
            sgu_o = _sgu_branch(x, mod, row0, seq, w_usv, sgu_g3, sgu_b3, sgu_w16, sgu_b_bc, l)
            attn_o = _attention(q, k, v, attn_sink, seq, l)
            x = _merge_sublayer(x, mod, row0, seq, attn_o, sgu_o, w_gates, w_bra, w_brs, w_mo, ln_g4, ln_b4, l)
            x = _ffn_sublayer(x, mod, row0, seq, ffn2_in, ffn2_out, ln_g4, ln_b4, l, 2)
        return x.reshape(bsz, seq, D_MODEL)

    return run(x_prompt, 0), run(x_sample, n_prompt)
```

```python
import functools

import jax
import jax.numpy as jnp
from jax import lax
from jax.experimental import pallas as pl
from jax.experimental.pallas import tpu as pltpu

D_MODEL = 2048
DEPTH = 2
HEAD_DIM = 128
N_Q_HEADS = 16
N_KV_HEADS = 4
Q_PER_KV = N_Q_HEADS // N_KV_HEADS
WINDOW = 128
ATTN_BLOCK = 128
ROPE_THETA = 10000.0
SGU_WIDTH = D_MODEL
SGU_CHUNK = 128
SGU_GROUPS = 16
D_FF = 5632
Q_WIDTH = N_Q_HEADS * HEAD_DIM
KV_WIDTH = N_KV_HEADS * HEAD_DIM
OFF_V = Q_WIDTH + KV_WIDTH
OFF_U = OFF_V + KV_WIDTH
OFF_GA = OFF_U + 2 * SGU_WIDTH
N_MOD = 9
DEEPNORM_ALPHA = (2 * DEPTH) ** 0.25
MACARON_WEIGHT = 0.5
LN_EPS = 1e-5
ATTN_SCALE = HEAD_DIM ** -0.5
LOG2E = 1.4426950408889634
Q_PRESCALE = ATTN_SCALE * LOG2E
ATTN_TAIL = 16

BF16 = jnp.bfloat16
F32 = jnp.float32

LANE = 128
SUBLANE = 8
TOKEN_TILE = 512
FFN_TOKEN_TILE = 1024
FF_TILE = 512
MERGE_TILE = 512
ADA_TILE = 1024
VMEM_LIMIT = 56 * 1024 * 1024


def _dot(a, b):
    return jnp.dot(a, b, preferred_element_type=F32)


def _sigmoid(x):
    return 1.0 / (1.0 + jnp.exp(-x))


def _gelu_tanh(x):
    return 0.5 * x * (1.0 + jnp.tanh(0.7978845608028654 * (x + 0.044715 * (x * x * x))))


def _layer_norm(r, g, b):
    mu = jnp.mean(r, axis=-1, keepdims=True)
    xc = r - mu
    var = jnp.mean(xc * xc, axis=-1, keepdims=True)
    return xc * lax.rsqrt(var + LN_EPS) * g + b


def _modulate(x_ref, mod_ref, sub):
    shift = mod_ref[3 * sub:3 * sub + 1, :]
    scale = mod_ref[3 * sub + 1:3 * sub + 2, :]
    return (x_ref[...] * (1.0 + scale) + shift).astype(BF16)


def _params(sem):
    return pltpu.CompilerParams(dimension_semantics=sem, vmem_limit_bytes=VMEM_LIMIT)


def _ada_kernel(c_ref, w_ref, b_ref, o_ref):
    c = c_ref[...]
    s = (c * _sigmoid(c)).astype(BF16)
    o_ref[...] = _dot(s, w_ref[...].astype(BF16)) + b_ref[...]


def _ada_mod(c_all, w_ada, b_ada):
    rows = c_all.shape[0]
    n_out = N_MOD * D_MODEL
    out = pl.pallas_call(
        _ada_kernel,
        grid=(DEPTH, n_out // ADA_TILE),
        in_specs=[
            pl.BlockSpec((rows, D_MODEL), lambda l, j: (0, 0)),
            pl.BlockSpec((None, D_MODEL, ADA_TILE), lambda l, j: (l, 0, j)),
            pl.BlockSpec((None, 1, ADA_TILE), lambda l, j: (l, 0, j)),
        ],
        out_specs=pl.BlockSpec((None, rows, ADA_TILE), lambda l, j: (l, 0, j)),
        out_shape=jax.ShapeDtypeStruct((DEPTH, rows, n_out), F32),
        compiler_params=_params(("arbitrary", "arbitrary")),
        name="ada_mod",
    )(c_all, w_ada, b_ada.reshape(DEPTH, 1, n_out))
    return out.reshape(DEPTH, rows, N_MOD, D_MODEL)


def _stream_prime(first_copies, second_copy):
    for c in first_copies(0, 0) + first_copies(1, 1):
        c.start()
    second_copy(0, 0).start()
    for c in first_copies(0, 0):
        c.wait()


def _stream_boundary(j, t0, n_chunks, has_next_tile, first_copies, second_copy):
    slot = (t0 + j) % 2
    second_copy(j, slot).wait()

    @pl.when((j + 1 < n_chunks) | has_next_tile)
    def _():
        for c in first_copies((j + 1) % n_chunks, 1 - slot):
            c.wait()
        second_copy((j + 1) % n_chunks, 1 - slot).start()

    @pl.when((j + 2 < n_chunks) | has_next_tile)
    def _():
        for c in first_copies((j + 2) % n_chunks, slot):
            c.start()
    return slot


def _ffn_kernel(x_hbm, mod_ref, win_hbm, wout_hbm, lng_ref, lnb_ref, o_ref,
                x_buf, h_ref, a_buf, wg_buf, wu_buf, wo_buf, x_sem, gu_sem, wo_sem, *, layer, sub, nf, tf):
    i, n_tiles = pl.program_id(0), pl.num_programs(0)
    tm = x_buf.shape[0]
    t0 = i * nf
    res_gate = MACARON_WEIGHT * mod_ref[3 * sub + 2:3 * sub + 3, :]
    halves = [slice(r * (tm // 2), (r + 1) * (tm // 2)) for r in range(2)]

    def x_copy(tile):
        return pltpu.make_async_copy(x_hbm.at[pl.ds(tile * tm, tm), :], x_buf, x_sem)

    def gu_copies(chunk, slot):
        col = pl.multiple_of(chunk * tf, tf)
        return (pltpu.make_async_copy(win_hbm.at[layer, :, pl.ds(col, tf)], wg_buf.at[slot], gu_sem.at[slot, 0]),
                pltpu.make_async_copy(win_hbm.at[layer, :, pl.ds(D_FF + col, tf)], wu_buf.at[slot],
                                      gu_sem.at[slot, 1]))

    def wo_copy(chunk, slot):
        row = pl.multiple_of(chunk * tf, tf)
        return pltpu.make_async_copy(wout_hbm.at[layer, pl.ds(row, tf), :], wo_buf.at[slot], wo_sem.at[slot])

    def gate_up(h, slot):
        g = _dot(h, wg_buf[slot])
        u = _dot(h, wu_buf[slot])
        return (g * _sigmoid(g) * u).astype(BF16)

    def boundary(j):
        return _stream_boundary(j, t0, nf, i + 1 < n_tiles, gu_copies, wo_copy)

    @pl.when(i == 0)
    def _():
        x_copy(0).start()
        _stream_prime(gu_copies, wo_copy)

    x_copy(i).wait()
    slot0 = t0 % 2
    shift = mod_ref[3 * sub:3 * sub + 1, :]
    scale = mod_ref[3 * sub + 1:3 * sub + 2, :]
    for rows in halves:
        x = x_buf[rows, :]
        h = (x * (1.0 + scale) + shift).astype(BF16)
        h_ref[rows, :] = h
        o_ref[rows, :] = DEEPNORM_ALPHA * x
        a_buf[slot0, rows, :] = gate_up(h, slot0)

    @pl.when(i + 1 < n_tiles)
    def _():
        x_copy(i + 1).start()

    def trip(j, carry):
        slot = boundary(j)
        o_ref[...] += res_gate * _dot(a_buf[slot], wo_buf[slot])
        a_buf[1 - slot] = gate_up(h_ref[...], 1 - slot)
        return carry

    lax.fori_loop(0, nf - 1, trip, 0)

    slot = boundary(nf - 1)
    res = [o_ref[rows, :] + res_gate * _dot(a_buf[slot, rows, :], wo_buf[slot]) for rows in halves]
    for rows, r in zip(halves, res):
        o_ref[rows, :] = _layer_norm(r, lng_ref[...], lnb_ref[...])


def _ffn_sublayer(x, mod, row0, seq, w_in, w_out, ln_g, ln_b, layer, sub):
    m = x.shape[0]
    tm, tf = FFN_TOKEN_TILE, FF_TILE
    nf = D_FF // tf
    assert nf >= 2 and m % tm == 0 and seq % tm == 0
    return pl.pallas_call(
        functools.partial(_ffn_kernel, layer=layer, sub=sub, nf=nf, tf=tf),
        grid=(m // tm,),
        in_specs=[
            pl.BlockSpec(memory_space=pl.ANY),
            pl.BlockSpec((None, None, N_MOD, D_MODEL), lambda i: (layer, row0 + (i * tm) // seq, 0, 0)),
            pl.BlockSpec(memory_space=pl.ANY),
            pl.BlockSpec(memory_space=pl.ANY),
            pl.BlockSpec((None, None, 1, D_MODEL), lambda i: (layer, sub, 0, 0)),
            pl.BlockSpec((None, None, 1, D_MODEL), lambda i: (layer, sub, 0, 0)),
        ],
        out_specs=pl.BlockSpec((tm, D_MODEL), lambda i: (i, 0)),
        out_shape=jax.ShapeDtypeStruct((m, D_MODEL), F32),
        scratch_shapes=[
            pltpu.VMEM((tm, D_MODEL), F32),
            pltpu.VMEM((tm, D_MODEL), BF16),
            pltpu.VMEM((2, tm, tf), BF16),
            pltpu.VMEM((2, D_MODEL, tf), BF16),
            pltpu.VMEM((2, D_MODEL, tf), BF16),
            pltpu.VMEM((2, tf, D_MODEL), BF16),
            pltpu.SemaphoreType.DMA(()),
            pltpu.SemaphoreType.DMA((2, 2)),
            pltpu.SemaphoreType.DMA((2,)),
        ],
        compiler_params=_params(("arbitrary",)),
        name=f"ffn_l{layer}_s{sub}_m{m}",
    )(x, mod, w_in, w_out, ln_g, ln_b)


def _qkv_kernel(x_ref, mod_ref, w_ref, cos_ref, sin_ref, q_ref, k_ref, v_ref):
    h = _modulate(x_ref, mod_ref, 1)
    cos = cos_ref[...]
    sin = sin_ref[...]
    chunk = 4 * HEAD_DIM
    for c in range((Q_WIDTH + 2 * KV_WIDTH) // chunk):
        z = _dot(h, w_ref[:, c * chunk:(c + 1) * chunk])
        if c * chunk >= OFF_V:
            v_ref[...] = z.astype(BF16)
            continue
        for t in range(4):
            zh = z[:, t * HEAD_DIM:(t + 1) * HEAD_DIM]
            r = zh * cos + pltpu.roll(zh, HEAD_DIM // 2, 1) * sin
            if c * chunk < Q_WIDTH:
                q_ref[:, c * chunk + t * HEAD_DIM:c * chunk + (t + 1) * HEAD_DIM] = (r * Q_PRESCALE).astype(BF16)
            else:
                k_ref[:, t * HEAD_DIM:(t + 1) * HEAD_DIM] = r.astype(BF16)


def _qkv_proj(x, mod, row0, seq, w_qkv, cos_t, sin_t, layer):
    m = x.shape[0]
    tm = TOKEN_TILE
    n = Q_WIDTH + 2 * KV_WIDTH
    return pl.pallas_call(
        _qkv_kernel,
        grid=(m // tm,),
        in_specs=[
            pl.BlockSpec((tm, D_MODEL), lambda i: (i, 0)),
            pl.BlockSpec((None, None, N_MOD, D_MODEL), lambda i: (layer, row0 + (i * tm) // seq, 0, 0)),
            pl.BlockSpec((None, D_MODEL, n), lambda i: (layer, 0, 0), pipeline_mode=pl.Buffered(1)),
            pl.BlockSpec((tm, HEAD_DIM), lambda i: (i % (seq // tm), 0)),
            pl.BlockSpec((tm, HEAD_DIM), lambda i: (i % (seq // tm), 0)),
        ],
        out_specs=[
            pl.BlockSpec((tm, Q_WIDTH), lambda i: (i, 0)),
            pl.BlockSpec((tm, KV_WIDTH), lambda i: (i, 0)),
            pl.BlockSpec((tm, KV_WIDTH), lambda i: (i, 0)),
        ],
        out_shape=[
            jax.ShapeDtypeStruct((m, Q_WIDTH), BF16),
            jax.ShapeDtypeStruct((m, KV_WIDTH), BF16),
            jax.ShapeDtypeStruct((m, KV_WIDTH), BF16),
        ],
        compiler_params=_params(("parallel",)),
        name=f"qkv_l{layer}_m{m}",
    )(x, mod, w_qkv, cos_t, sin_t)


def _sgu_kernel(x_ref, mod_ref, w_ref, lng_ref, lnb_ref, ws_ref, bs_ref, o_ref, vn_ref):
    tm = x_ref.shape[0]
    h = _modulate(x_ref, mod_ref, 1)
    sv = _gelu_tanh(_dot(h, w_ref[:, SGU_WIDTH:2 * SGU_WIDTH]))
    vn_ref[...] = _layer_norm(sv, lng_ref[...], lnb_ref[...]).astype(BF16)
    nchunk = tm // SGU_CHUNK
    gpc = 4
    for c in range(SGU_GROUPS // gpc):
        lo = c * gpc * LANE
        u = _gelu_tanh(_dot(h, w_ref[:, lo:lo + gpc * LANE]))
        for t in range(gpc):
            g = c * gpc + t
            col = slice(g * LANE, (g + 1) * LANE)
            rhs = jnp.concatenate([vn_ref[n * SGU_CHUNK:(n + 1) * SGU_CHUNK, col] for n in range(nchunk)], axis=1)
            z = _dot(ws_ref[g], rhs)
            for n in range(nchunk):
                rows = slice(n * SGU_CHUNK, (n + 1) * SGU_CHUNK)
                zz = z[:, n * LANE:(n + 1) * LANE] + bs_ref[g]
                o_ref[rows, col] = (u[rows, t * LANE:(t + 1) * LANE] * zz).astype(BF16)


def _sgu_branch(x, mod, row0, seq, w_usv, sgu_ln_g, sgu_ln_b, sgu_w, sgu_b_bc, layer):
    m = x.shape[0]
    tm = TOKEN_TILE
    return pl.pallas_call(
        _sgu_kernel,
        grid=(m // tm,),
        in_specs=[
            pl.BlockSpec((tm, D_MODEL), lambda i: (i, 0)),
            pl.BlockSpec((None, None, N_MOD, D_MODEL), lambda i: (layer, row0 + (i * tm) // seq, 0, 0)),
            pl.BlockSpec((None, D_MODEL, 2 * SGU_WIDTH), lambda i: (layer, 0, 0), pipeline_mode=pl.Buffered(1)),
            pl.BlockSpec((None, 1, SGU_WIDTH), lambda i: (layer, 0, 0)),
            pl.BlockSpec((None, 1, SGU_WIDTH), lambda i: (layer, 0, 0)),
            pl.BlockSpec((None, SGU_GROUPS, SGU_CHUNK, SGU_CHUNK), lambda i: (layer, 0, 0, 0)),
            pl.BlockSpec((None, SGU_GROUPS, SGU_CHUNK, LANE), lambda i: (layer, 0, 0, 0)),
        ],
        out_specs=pl.BlockSpec((tm, SGU_WIDTH), lambda i: (i, 0)),
        out_shape=jax.ShapeDtypeStruct((m, SGU_WIDTH), BF16),
        scratch_shapes=[pltpu.VMEM((tm, SGU_WIDTH), BF16)],
        compiler_params=_params(("parallel",)),
        name=f"sgu_l{layer}_m{m}",
    )(x, mod, w_usv, sgu_ln_g, sgu_ln_b, sgu_w, sgu_b_bc)


def _attn_kernel(sink_ref, q_ref, kp_ref, kc_ref, kn_ref, vp_ref, vc_ref, vn_ref, o_ref, *, layer, seq):
    tq = q_ref.shape[0]
    pos0 = (pl.program_id(0) * tq) % seq
    kall = jnp.concatenate([kp_ref[...], kc_ref[...], kn_ref[...]], axis=0)
    vall = jnp.concatenate([vp_ref[...], vc_ref[...], vn_ref[...]], axis=0)
    span = 3 * ATTN_BLOCK
    kj = lax.broadcasted_iota(jnp.int32, (span, ATTN_BLOCK), 0)
    qi = lax.broadcasted_iota(jnp.int32, (span, ATTN_BLOCK), 1)
    band = jnp.abs(kj - ATTN_BLOCK - qi) <= WINDOW
    tail_row = lax.broadcasted_iota(jnp.int32, (ATTN_TAIL, 2 * HEAD_DIM), 0)
    tail_col = lax.broadcasted_iota(jnp.int32, (ATTN_TAIL, 2 * HEAD_DIM), 1)
    v_tail = jnp.where((tail_row < 2) & (tail_col >= HEAD_DIM), 1.0, 0.0).astype(BF16)
    ones_blk = jnp.ones((span, HEAD_DIM), BF16)
    p_row = lax.broadcasted_iota(jnp.int32, (ATTN_TAIL, Q_PER_KV * ATTN_BLOCK), 0)
    for n in range(tq // ATTN_BLOCK):
        kpos = pos0 + (n - 1) * ATTN_BLOCK + kj
        ok = band & (kpos >= 0) & (kpos < seq)
        bias = jnp.where(ok, 0.0, -jnp.inf)
        bias = jnp.concatenate([bias] * Q_PER_KV, axis=1)
        bias_prev, bias_next = bias[:ATTN_BLOCK], bias[2 * ATTN_BLOCK:]
        rows = slice(n * ATTN_BLOCK, (n + 1) * ATTN_BLOCK)
        for g in range(N_KV_HEADS):
            kg = kall[n * ATTN_BLOCK:n * ATTN_BLOCK + span, g * HEAD_DIM:(g + 1) * HEAD_DIM]
            vg = vall[n * ATTN_BLOCK:n * ATTN_BLOCK + span, g * HEAD_DIM:(g + 1) * HEAD_DIM]
            v_aug = jnp.concatenate([jnp.concatenate([vg, ones_blk], axis=1), v_tail], axis=0)
            heads = [g * Q_PER_KV + r for r in range(Q_PER_KV)]
            qs = jnp.concatenate([q_ref[rows, hh * HEAD_DIM:(hh + 1) * HEAD_DIM] for hh in heads], axis=0)
            s = lax.dot_general(kg, qs, (((1,), (1,)), ((), ())), preferred_element_type=F32)
            blocks = [s[:ATTN_BLOCK] + bias_prev, s[ATTN_BLOCK:2 * ATTN_BLOCK], s[2 * ATTN_BLOCK:] + bias_next]
            sink = jnp.concatenate(
                [jnp.full((1, ATTN_BLOCK), sink_ref[layer, hh] * LOG2E, F32) for hh in heads], axis=1)
            mx = sink
            for sb in blocks:
                mx = jnp.maximum(mx, jnp.max(sb, axis=0, keepdims=True))
            es = jnp.exp2(sink - mx)
            es_hi = es.astype(BF16).astype(F32)
            p_tail = jnp.where(p_row == 0, es_hi, jnp.where(p_row == 1, es - es_hi, 0.0)).astype(BF16)
            p_aug = jnp.concatenate([jnp.exp2(sb - mx).astype(BF16) for sb in blocks] + [p_tail], axis=0)
            o = lax.dot_general(p_aug, v_aug, (((0,), (0,)), ((), ())), preferred_element_type=F32)
            o = o[:, :HEAD_DIM] / o[:, HEAD_DIM:]
            for r, hh in enumerate(heads):
                o_ref[rows, hh * HEAD_DIM:(hh + 1) * HEAD_DIM] = (
                    o[r * ATTN_BLOCK:(r + 1) * ATTN_BLOCK].astype(BF16))


def _attention(q, k, v, attn_sink, seq, layer):
    m = q.shape[0]
    tq = TOKEN_TILE
    bpt = tq // ATTN_BLOCK
    last = m // ATTN_BLOCK - 1
    prev_map = lambda i: (jnp.maximum(i * bpt - 1, 0), 0)
    next_map = lambda i: (jnp.minimum(i * bpt + bpt, last), 0)
    cur_map = lambda i: (i, 0)
    return pl.pallas_call(
        functools.partial(_attn_kernel, layer=layer, seq=seq),
        grid=(m // tq,),
        in_specs=[
            pl.BlockSpec(memory_space=pltpu.SMEM),
            pl.BlockSpec((tq, Q_WIDTH), cur_map),
            pl.BlockSpec((ATTN_BLOCK, KV_WIDTH), prev_map),
            pl.BlockSpec((tq, KV_WIDTH), cur_map),
            pl.BlockSpec((ATTN_BLOCK, KV_WIDTH), next_map),
            pl.BlockSpec((ATTN_BLOCK, KV_WIDTH), prev_map),
            pl.BlockSpec((tq, KV_WIDTH), cur_map),
            pl.BlockSpec((ATTN_BLOCK, KV_WIDTH), next_map),
        ],
        out_specs=pl.BlockSpec((tq, Q_WIDTH), cur_map),
        out_shape=jax.ShapeDtypeStruct((m, Q_WIDTH), BF16),
        compiler_params=_params(("parallel",)),
        name=f"attn_l{layer}_m{m}",
    )(attn_sink, q, k, k, k, v, v, v)


def _merge_kernel(x_ref, mod_ref, ao_ref, so_ref, wgate_hbm, wba_hbm, wbs_hbm, wmo_hbm, lng_ref, lnb_ref, o_ref,
                  h_ref, m_buf, wga_buf, wgb_buf, wba_buf, wbs_buf, wmo_buf, br_sem, mo_sem, *, layer, nj, tn):
    i, n_tiles = pl.program_id(0), pl.num_programs(0)
    t0 = i * nj
    gate = mod_ref[5:6, :]

    def br_copies(chunk, slot):
        col = pl.multiple_of(chunk * tn, tn)
        return (
            pltpu.make_async_copy(wgate_hbm.at[layer, :, pl.ds(col, tn)], wga_buf.at[slot], br_sem.at[slot, 0]),
            pltpu.make_async_copy(wgate_hbm.at[layer, :, pl.ds(D_MODEL + col, tn)], wgb_buf.at[slot],
                                  br_sem.at[slot, 1]),
            pltpu.make_async_copy(wba_hbm.at[layer, :, pl.ds(col, tn)], wba_buf.at[slot], br_sem.at[slot, 2]),
            pltpu.make_async_copy(wbs_hbm.at[layer, :, pl.ds(col, tn)], wbs_buf.at[slot], br_sem.at[slot, 3]),
        )

    def mo_copy(chunk, slot):
        row = pl.multiple_of(chunk * tn, tn)
        return pltpu.make_async_copy(wmo_hbm.at[layer, pl.ds(row, tn), :], wmo_buf.at[slot], mo_sem.at[slot])

    def merged_chunk(slot):
        h = h_ref[...]
        attn = _dot(ao_ref[...], wba_buf[slot])
        sgu = _dot(so_ref[...], wbs_buf[slot])
        merged = _sigmoid(_dot(h, wga_buf[slot])) * attn + _sigmoid(_dot(h, wgb_buf[slot])) * sgu
        return merged.astype(BF16)

    def boundary(j):
        return _stream_boundary(j, t0, nj, i + 1 < n_tiles, br_copies, mo_copy)

    @pl.when(i == 0)
    def _():
        _stream_prime(br_copies, mo_copy)

    slot0 = t0 % 2
    h_ref[...] = _modulate(x_ref, mod_ref, 1)
    o_ref[...] = DEEPNORM_ALPHA * x_ref[...]
    m_buf[slot0] = merged_chunk(slot0)

    def trip(j, carry):
        slot = boundary(j)
        o_ref[...] += gate * _dot(m_buf[slot], wmo_buf[slot])
        m_buf[1 - slot] = merged_chunk(1 - slot)
        return carry

    lax.fori_loop(0, nj - 1, trip, 0)

    slot = boundary(nj - 1)
    r = o_ref[...] + gate * _dot(m_buf[slot], wmo_buf[slot])
    o_ref[...] = _layer_norm(r, lng_ref[...], lnb_ref[...])


def _merge_sublayer(x, mod, row0, seq, attn_o, sgu_o, w_gates, w_br_attn, w_br_sgu, w_mix_out, ln_g, ln_b, layer):
    m = x.shape[0]
    tm, tn = TOKEN_TILE, MERGE_TILE
    nj = D_MODEL // tn
    assert nj >= 2 and m % tm == 0 and seq % tm == 0
    return pl.pallas_call(
        functools.partial(_merge_kernel, layer=layer, nj=nj, tn=tn),
        grid=(m // tm,),
        in_specs=[
            pl.BlockSpec((tm, D_MODEL), lambda i: (i, 0)),
            pl.BlockSpec((None, None, N_MOD, D_MODEL), lambda i: (layer, row0 + (i * tm) // seq, 0, 0)),
            pl.BlockSpec((tm, Q_WIDTH), lambda i: (i, 0)),
            pl.BlockSpec((tm, SGU_WIDTH), lambda i: (i, 0)),
            pl.BlockSpec(memory_space=pl.ANY),
            pl.BlockSpec(memory_space=pl.ANY),
            pl.BlockSpec(memory_space=pl.ANY),
            pl.BlockSpec(memory_space=pl.ANY),
            pl.BlockSpec((None, None, 1, D_MODEL), lambda i: (layer, 1, 0, 0)),
            pl.BlockSpec((None, None, 1, D_MODEL), lambda i: (layer, 1, 0, 0)),
        ],
        out_specs=pl.BlockSpec((tm, D_MODEL), lambda i: (i, 0)),
        out_shape=jax.ShapeDtypeStruct((m, D_MODEL), F32),
        scratch_shapes=[
            pltpu.VMEM((tm, D_MODEL), BF16),
            pltpu.VMEM((2, tm, tn), BF16),
            pltpu.VMEM((2, D_MODEL, tn), BF16),
            pltpu.VMEM((2, D_MODEL, tn), BF16),
            pltpu.VMEM((2, Q_WIDTH, tn), BF16),
            pltpu.VMEM((2, SGU_WIDTH, tn), BF16),
            pltpu.VMEM((2, tn, D_MODEL), BF16),
            pltpu.SemaphoreType.DMA((2, 4)),
            pltpu.SemaphoreType.DMA((2,)),
        ],
        compiler_params=_params(("arbitrary",)),
        name=f"merge_l{layer}_m{m}",
    )(x, mod, attn_o, sgu_o, w_gates, w_br_attn, w_br_sgu, w_mix_out, ln_g, ln_b)


def _rope_tables(seq):
    half = HEAD_DIM // 2
    inv_freq = 1.0 / (ROPE_THETA ** (jnp.arange(half, dtype=F32) / half))
    ang = jnp.arange(seq, dtype=F32)[:, None] * inv_freq[None, :]
    cos = jnp.cos(ang)
    sin = jnp.sin(ang)
    return jnp.concatenate([cos, cos], axis=1), jnp.concatenate([-sin, sin], axis=1)


def kernel(x_prompt, x_sample, c_prompt, c_sample, w_ada, b_ada, ln_g, ln_b, ffn1_w_in, ffn1_w_out, w_mix_in, attn_sink, sgu_ln_g, sgu_ln_b, sgu_w, sgu_b, w_br_attn, w_br_sgu, w_mix_out, ffn2_w_in, ffn2_w_out):
    n_prompt = c_prompt.shape[0]
    c_all = jnp.concatenate([c_prompt, c_sample], axis=0)
    c_all = jnp.pad(c_all, ((0, -c_all.shape[0] % SUBLANE), (0, 0)))
    mod = _ada_mod(c_all, w_ada, b_ada)

    ffn1_in, ffn1_out = ffn1_w_in.astype(BF16), ffn1_w_out.astype(BF16)
    ffn2_in, ffn2_out = ffn2_w_in.astype(BF16), ffn2_w_out.astype(BF16)
    w_qkv = w_mix_in[:, :, :OFF_U].astype(BF16)
    w_usv = w_mix_in[:, :, OFF_U:OFF_GA].astype(BF16)
    w_gates = w_mix_in[:, :, OFF_GA:].astype(BF16)
    w_bra, w_brs, w_mo = w_br_attn.astype(BF16), w_br_sgu.astype(BF16), w_mix_out.astype(BF16)
    sgu_w16 = sgu_w.astype(BF16)
    sgu_b_bc = jnp.broadcast_to(sgu_b[..., None], sgu_b.shape + (LANE,))
    ln_g4 = ln_g.reshape(DEPTH, 3, 1, D_MODEL)
    ln_b4 = ln_b.reshape(DEPTH, 3, 1, D_MODEL)
    sgu_g3 = sgu_ln_g.reshape(DEPTH, 1, SGU_WIDTH)
    sgu_b3 = sgu_ln_b.reshape(DEPTH, 1, SGU_WIDTH)

    def run(x3, row0):
        bsz, seq, _ = x3.shape
        x = x3.reshape(bsz * seq, D_MODEL)
        cos_t, sin_t = _rope_tables(seq)
        for l in range(DEPTH):
            x = _ffn_sublayer(x, mod, row0, seq, ffn1_in, ffn1_out, ln_g4, ln_b4, l, 0)
            q, k, v = _qkv_proj(x, mod, row0, seq, w_qkv, cos_t, sin_t, l)
            sgu_o = _sgu_branch(x, mod, row0, seq, w_usv, sgu_g3, sgu_b3, sgu_w16, sgu_b_bc, l)
            attn_o = _attention(q, k, v, attn_sink, seq, l)
            x = _merge_sublayer(x, mod, row0, seq, attn_o, sgu_o, w_gates, w_bra, w_brs, w_mo, ln_g4, ln_b4, l)
            x = _ffn_sublayer(x, mod, row0, seq, ffn2_in, ffn2_out, ln_g4, ln_b4, l, 2)
        return x.reshape(bsz, seq, D_MODEL)

    return run(x_prompt, 0), run(x_sample, n_prompt)
```

```python
import functools

import jax
import jax.numpy as jnp
from jax import lax
from jax.experimental import pallas as pl
from jax.experimental.pallas import tpu as pltpu

D_MODEL = 2048
DEPTH = 2
HEAD_DIM = 128
N_Q_HEADS = 16
N_KV_HEADS = 4
Q_PER_KV = N_Q_HEADS // N_KV_HEADS
WINDOW = 128
ATTN_BLOCK = 128
ROPE_THETA = 10000.0
SGU_WIDTH = D_MODEL
SGU_CHUNK = 128
SGU_GROUPS = 16
D_FF = 5632
Q_WIDTH = N_Q_HEADS * HEAD_DIM
KV_WIDTH = N_KV_HEADS * HEAD_DIM
OFF_V = Q_WIDTH + KV_WIDTH
OFF_U = OFF_V + KV_WIDTH
OFF_GA = OFF_U + 2 * SGU_WIDTH
N_MOD = 9
DEEPNORM_ALPHA = (2 * DEPTH) ** 0.25
MACARON_WEIGHT = 0.5
LN_EPS = 1e-5
ATTN_SCALE = HEAD_DIM ** -0.5
LOG2E = 1.4426950408889634
Q_PRESCALE = ATTN_SCALE * LOG2E
SGU_LOOKAHEAD = 2
ATTN_LOOKAHEAD = 1
ATTN_TAIL = 16

BF16 = jnp.bfloat16
F32 = jnp.float32

LANE = 128
SUBLANE = 8
TOKEN_TILE = 512
FFN_TOKEN_TILE = 1024
FF_TILE = 512
MERGE_TILE = 512
ADA_TILE = 1024
VMEM_LIMIT = 56 * 1024 * 1024


def _dot(a, b):
    return jnp.dot(a, b, preferred_element_type=F32)


def _sigmoid(x):
    return 1.0 / (1.0 + jnp.exp(-x))


def _gelu_tanh(x):
    return 0.5 * x * (1.0 + jnp.tanh(0.7978845608028654 * (x + 0.044715 * (x * x * x))))


def _layer_norm(r, g, b):
    mu = jnp.mean(r, axis=-1, keepdims=True)
    xc = r - mu
    var = jnp.mean(xc * xc, axis=-1, keepdims=True)
    return xc * lax.rsqrt(var + LN_EPS) * g + b


def _modulate(x_ref, mod_ref, sub):
    shift = mod_ref[3 * sub:3 * sub + 1, :]
    scale = mod_ref[3 * sub + 1:3 * sub + 2, :]
    return (x_ref[...] * (1.0 + scale) + shift).astype(BF16)


def _params(sem):
    return pltpu.CompilerParams(dimension_semantics=sem, vmem_limit_bytes=VMEM_LIMIT)


def _ada_kernel(c_ref, w_ref, b_ref, o_ref):
    c = c_ref[...]
    s = (c * _sigmoid(c)).astype(BF16)
    o_ref[...] = _dot(s, w_ref[...].astype(BF16)) + b_ref[...]


def _ada_mod(c_all, w_ada, b_ada):
    rows = c_all.shape[0]
    n_out = N_MOD * D_MODEL
    out = pl.pallas_call(
        _ada_kernel,
        grid=(DEPTH, n_out // ADA_TILE),
        in_specs=[
            pl.BlockSpec((rows, D_MODEL), lambda l, j: (0, 0)),
            pl.BlockSpec((None, D_MODEL, ADA_TILE), lambda l, j: (l, 0, j)),
            pl.BlockSpec((None, 1, ADA_TILE), lambda l, j: (l, 0, j)),
        ],
        out_specs=pl.BlockSpec((None, rows, ADA_TILE), lambda l, j: (l, 0, j)),
        out_shape=jax.ShapeDtypeStruct((DEPTH, rows, n_out), F32),
        compiler_params=_params(("arbitrary", "arbitrary")),
        name="ada_mod",
    )(c_all, w_ada, b_ada.reshape(DEPTH, 1, n_out))
    return out.reshape(DEPTH, rows, N_MOD, D_MODEL)


def _stream_prime(first_copies, second_copy):
    for c in first_copies(0, 0) + first_copies(1, 1):
        c.start()
    second_copy(0, 0).start()
    for c in first_copies(0, 0):
        c.wait()


def _stream_boundary(j, t0, n_chunks, has_next_tile, first_copies, second_copy):
    slot = (t0 + j) % 2
    second_copy(j, slot).wait()

    @pl.when((j + 1 < n_chunks) | has_next_tile)
    def _():
        for c in first_copies((j + 1) % n_chunks, 1 - slot):
            c.wait()
        second_copy((j + 1) % n_chunks, 1 - slot).start()

    @pl.when((j + 2 < n_chunks) | has_next_tile)
    def _():
        for c in first_copies((j + 2) % n_chunks, slot):
            c.start()
    return slot


def _ffn_kernel(x_hbm, mod_ref, win_hbm, wout_hbm, lng_ref, lnb_ref, o_ref,
                x_buf, h_ref, a_buf, wg_buf, wu_buf, wo_buf, x_sem, gu_sem, wo_sem, *, layer, sub, nf, tf):
    i, n_tiles = pl.program_id(0), pl.num_programs(0)
    tm = x_buf.shape[0]
    t0 = i * nf
    res_gate = MACARON_WEIGHT * mod_ref[3 * sub + 2:3 * sub + 3, :]
    halves = [slice(r * (tm // 2), (r + 1) * (tm // 2)) for r in range(2)]

    def x_copy(tile):
        return pltpu.make_async_copy(x_hbm.at[pl.ds(tile * tm, tm), :], x_buf, x_sem)

    def gu_copies(chunk, slot):
        col = pl.multiple_of(chunk * tf, tf)
        return (pltpu.make_async_copy(win_hbm.at[layer, :, pl.ds(col, tf)], wg_buf.at[slot], gu_sem.at[slot, 0]),
                pltpu.make_async_copy(win_hbm.at[layer, :, pl.ds(D_FF + col, tf)], wu_buf.at[slot],
                                      gu_sem.at[slot, 1]))

    def wo_copy(chunk, slot):
        row = pl.multiple_of(chunk * tf, tf)
        return pltpu.make_async_copy(wout_hbm.at[layer, pl.ds(row, tf), :], wo_buf.at[slot], wo_sem.at[slot])

    def gate_up(h, slot):
        g = _dot(h, wg_buf[slot])
        u = _dot(h, wu_buf[slot])
        return (g * _sigmoid(g) * u).astype(BF16)

    def boundary(j):
        return _stream_boundary(j, t0, nf, i + 1 < n_tiles, gu_copies, wo_copy)

    @pl.when(i == 0)
    def _():
        x_copy(0).start()
        _stream_prime(gu_copies, wo_copy)

    x_copy(i).wait()
    slot0 = t0 % 2
    shift = mod_ref[3 * sub:3 * sub + 1, :]
    scale = mod_ref[3 * sub + 1:3 * sub + 2, :]
    for rows in halves:
        x = x_buf[rows, :]
        h = (x * (1.0 + scale) + shift).astype(BF16)
        h_ref[rows, :] = h
        o_ref[rows, :] = DEEPNORM_ALPHA * x
        a_buf[slot0, rows, :] = gate_up(h, slot0)

    @pl.when(i + 1 < n_tiles)
    def _():
        x_copy(i + 1).start()

    def trip(j, carry):
        slot = boundary(j)
        o_ref[...] += res_gate * _dot(a_buf[slot], wo_buf[slot])
        a_buf[1 - slot] = gate_up(h_ref[...], 1 - slot)
        return carry

    lax.fori_loop(0, nf - 1, trip, 0)

    slot = boundary(nf - 1)
    res = [o_ref[rows, :] + res_gate * _dot(a_buf[slot, rows, :], wo_buf[slot]) for rows in halves]
    for rows, r in zip(halves, res):
        o_ref[rows, :] = _layer_norm(r, lng_ref[...], lnb_ref[...])


def _ffn_sublayer(x, mod, row0, seq, w_in, w_out, ln_g, ln_b, layer, sub):
    m = x.shape[0]
    tm, tf = FFN_TOKEN_TILE, FF_TILE
    nf = D_FF // tf
    assert nf >= 2 and m % tm == 0 and seq % tm == 0
    return pl.pallas_call(
        functools.partial(_ffn_kernel, layer=layer, sub=sub, nf=nf, tf=tf),
        grid=(m // tm,),
        in_specs=[
            pl.BlockSpec(memory_space=pl.ANY),
            pl.BlockSpec((None, None, N_MOD, D_MODEL), lambda i: (layer, row0 + (i * tm) // seq, 0, 0)),
            pl.BlockSpec(memory_space=pl.ANY),
            pl.BlockSpec(memory_space=pl.ANY),
            pl.BlockSpec((None, None, 1, D_MODEL), lambda i: (layer, sub, 0, 0)),
            pl.BlockSpec((None, None, 1, D_MODEL), lambda i: (layer, sub, 0, 0)),
        ],
        out_specs=pl.BlockSpec((tm, D_MODEL), lambda i: (i, 0)),
        out_shape=jax.ShapeDtypeStruct((m, D_MODEL), F32),
        scratch_shapes=[
            pltpu.VMEM((tm, D_MODEL), F32),
            pltpu.VMEM((tm, D_MODEL), BF16),
            pltpu.VMEM((2, tm, tf), BF16),
            pltpu.VMEM((2, D_MODEL, tf), BF16),
            pltpu.VMEM((2, D_MODEL, tf), BF16),
            pltpu.VMEM((2, tf, D_MODEL), BF16),
            pltpu.SemaphoreType.DMA(()),
            pltpu.SemaphoreType.DMA((2, 2)),
            pltpu.SemaphoreType.DMA((2,)),
        ],
        compiler_params=_params(("arbitrary",)),
        name=f"ffn_l{layer}_s{sub}_m{m}",
    )(x, mod, w_in, w_out, ln_g, ln_b)


def _qkv_kernel(x_ref, mod_ref, w_ref, cos_ref, sin_ref, q_ref, k_ref, v_ref):
    h = _modulate(x_ref, mod_ref, 1)
    cos = cos_ref[...]
    sin = sin_ref[...]
    chunk = 4 * HEAD_DIM
    for c in range((Q_WIDTH + 2 * KV_WIDTH) // chunk):
        z = _dot(h, w_ref[:, c * chunk:(c + 1) * chunk])
        if c * chunk >= OFF_V:
            v_ref[...] = z.astype(BF16)
            continue
        for t in range(4):
            zh = z[:, t * HEAD_DIM:(t + 1) * HEAD_DIM]
            r = zh * cos + pltpu.roll(zh, HEAD_DIM // 2, 1) * sin
            if c * chunk < Q_WIDTH:
                q_ref[:, c * chunk + t * HEAD_DIM:c * chunk + (t + 1) * HEAD_DIM] = (r * Q_PRESCALE).astype(BF16)
            else:
                k_ref[:, t * HEAD_DIM:(t + 1) * HEAD_DIM] = r.astype(BF16)


def _qkv_proj(x, mod, row0, seq, w_qkv, cos_t, sin_t, layer):
    m = x.shape[0]
    tm = TOKEN_TILE
    n = Q_WIDTH + 2 * KV_WIDTH
    return pl.pallas_call(
        _qkv_kernel,
        grid=(m // tm,),
        in_specs=[
            pl.BlockSpec((tm, D_MODEL), lambda i: (i, 0)),
            pl.BlockSpec((None, None, N_MOD, D_MODEL), lambda i: (layer, row0 + (i * tm) // seq, 0, 0)),
            pl.BlockSpec((None, D_MODEL, n), lambda i: (layer, 0, 0), pipeline_mode=pl.Buffered(1)),
            pl.BlockSpec((tm, HEAD_DIM), lambda i: (i % (seq // tm), 0)),
            pl.BlockSpec((tm, HEAD_DIM), lambda i: (i % (seq // tm), 0)),
        ],
        out_specs=[
            pl.BlockSpec((tm, Q_WIDTH), lambda i: (i, 0)),
            pl.BlockSpec((tm, KV_WIDTH), lambda i: (i, 0)),
            pl.BlockSpec((tm, KV_WIDTH), lambda i: (i, 0)),
        ],
        out_shape=[
            jax.ShapeDtypeStruct((m, Q_WIDTH), BF16),
            jax.ShapeDtypeStruct((m, KV_WIDTH), BF16),
            jax.ShapeDtypeStruct((m, KV_WIDTH), BF16),
        ],
        compiler_params=_params(("parallel",)),
        name=f"qkv_l{layer}_m{m}",
    )(x, mod, w_qkv, cos_t, sin_t)


def _sgu_kernel(x_ref, mod_ref, w_ref, lng_ref, lnb_ref, ws_ref, bs_ref, o_ref, vn_ref):
    tm = x_ref.shape[0]
    h = _modulate(x_ref, mod_ref, 1)
    nchunk = tm // SGU_CHUNK
    gpc = 4

    def u_chunk(c):
        return _gelu_tanh(_dot(h, w_ref[:, c * gpc * LANE:(c + 1) * gpc * LANE]))

    sv = _gelu_tanh(_dot(h, w_ref[:, SGU_WIDTH:2 * SGU_WIDTH]))
    pending = [u_chunk(c) for c in range(SGU_LOOKAHEAD)]
    vn_ref[...] = _layer_norm(sv, lng_ref[...], lnb_ref[...]).astype(BF16)
    for c in range(SGU_GROUPS // gpc):
        if c + SGU_LOOKAHEAD < SGU_GROUPS // gpc:
            pending.append(u_chunk(c + SGU_LOOKAHEAD))
        u = pending.pop(0)
        for t in range(gpc):
            g = c * gpc + t
            col = slice(g * LANE, (g + 1) * LANE)
            rhs = jnp.concatenate([vn_ref[n * SGU_CHUNK:(n + 1) * SGU_CHUNK, col] for n in range(nchunk)], axis=1)
            z = _dot(ws_ref[g], rhs)
            for n in range(nchunk):
                rows = slice(n * SGU_CHUNK, (n + 1) * SGU_CHUNK)
                zz = z[:, n * LANE:(n + 1) * LANE] + bs_ref[g]
                o_ref[rows, col] = (u[rows, t * LANE:(t + 1) * LANE] * zz).astype(BF16)


def _sgu_branch(x, mod, row0, seq, w_usv, sgu_ln_g, sgu_ln_b, sgu_w, sgu_b_bc, layer):
    m = x.shape[0]
    tm = TOKEN_TILE
    return pl.pallas_call(
        _sgu_kernel,
        grid=(m // tm,),
        in_specs=[
            pl.BlockSpec((tm, D_MODEL), lambda i: (i, 0)),
            pl.BlockSpec((None, None, N_MOD, D_MODEL), lambda i: (layer, row0 + (i * tm) // seq, 0, 0)),
            pl.BlockSpec((None, D_MODEL, 2 * SGU_WIDTH), lambda i: (layer, 0, 0), pipeline_mode=pl.Buffered(1)),
            pl.BlockSpec((None, 1, SGU_WIDTH), lambda i: (layer, 0, 0)),
            pl.BlockSpec((None, 1, SGU_WIDTH), lambda i: (layer, 0, 0)),
            pl.BlockSpec((None, SGU_GROUPS, SGU_CHUNK, SGU_CHUNK), lambda i: (layer, 0, 0, 0)),
            pl.BlockSpec((None, SGU_GROUPS, SGU_CHUNK, LANE), lambda i: (layer, 0, 0, 0)),
        ],
        out_specs=pl.BlockSpec((tm, SGU_WIDTH), lambda i: (i, 0)),
        out_shape=jax.ShapeDtypeStruct((m, SGU_WIDTH), BF16),
        scratch_shapes=[pltpu.VMEM((tm, SGU_WIDTH), BF16)],
        compiler_params=_params(("parallel",)),
        name=f"sgu_l{layer}_m{m}",
    )(x, mod, w_usv, sgu_ln_g, sgu_ln_b, sgu_w, sgu_b_bc)


def _attn_kernel(sink_ref, q_ref, kp_ref, kc_ref, kn_ref, vp_ref, vc_ref, vn_ref, o_ref, *, layer, seq):
    tq = q_ref.shape[0]
    pos0 = (pl.program_id(0) * tq) % seq
    kall = jnp.concatenate([kp_ref[...], kc_ref[...], kn_ref[...]], axis=0)
    vall = jnp.concatenate([vp_ref[...], vc_ref[...], vn_ref[...]], axis=0)
    span = 3 * ATTN_BLOCK
    kj = lax.broadcasted_iota(jnp.int32, (span, ATTN_BLOCK), 0)
    qi = lax.broadcasted_iota(jnp.int32, (span, ATTN_BLOCK), 1)
    band = jnp.abs(kj - ATTN_BLOCK - qi) <= WINDOW
    tail_row = lax.broadcasted_iota(jnp.int32, (ATTN_TAIL, 2 * HEAD_DIM), 0)
    tail_col = lax.broadcasted_iota(jnp.int32, (ATTN_TAIL, 2 * HEAD_DIM), 1)
    v_tail = jnp.where((tail_row < 2) & (tail_col >= HEAD_DIM), 1.0, 0.0).astype(BF16)
    ones_blk = jnp.ones((span, HEAD_DIM), BF16)
    p_row = lax.broadcasted_iota(jnp.int32, (ATTN_TAIL, Q_PER_KV * ATTN_BLOCK), 0)
    def band_bias(n):
        kpos = pos0 + (n - 1) * ATTN_BLOCK + kj
        ok = band & (kpos >= 0) & (kpos < seq)
        bias = jnp.where(ok, 0.0, -jnp.inf)
        bias = jnp.concatenate([bias] * Q_PER_KV, axis=1)
        return bias[:ATTN_BLOCK], bias[2 * ATTN_BLOCK:]

    def scores(n, g):
        rows = slice(n * ATTN_BLOCK, (n + 1) * ATTN_BLOCK)
        kg = kall[n * ATTN_BLOCK:n * ATTN_BLOCK + span, g * HEAD_DIM:(g + 1) * HEAD_DIM]
        heads = [g * Q_PER_KV + r for r in range(Q_PER_KV)]
        qs = jnp.concatenate([q_ref[rows, hh * HEAD_DIM:(hh + 1) * HEAD_DIM] for hh in heads], axis=0)
        return lax.dot_general(kg, qs, (((1,), (1,)), ((), ())), preferred_element_type=F32)

    def finish(n, g, s, bias_prev, bias_next):
        rows = slice(n * ATTN_BLOCK, (n + 1) * ATTN_BLOCK)
        vg = vall[n * ATTN_BLOCK:n * ATTN_BLOCK + span, g * HEAD_DIM:(g + 1) * HEAD_DIM]
        v_aug = jnp.concatenate([jnp.concatenate([vg, ones_blk], axis=1), v_tail], axis=0)
        heads = [g * Q_PER_KV + r for r in range(Q_PER_KV)]
        blocks = [s[:ATTN_BLOCK] + bias_prev, s[ATTN_BLOCK:2 * ATTN_BLOCK], s[2 * ATTN_BLOCK:] + bias_next]
        sink = jnp.concatenate(
            [jnp.full((1, ATTN_BLOCK), sink_ref[layer, hh] * LOG2E, F32) for hh in heads], axis=1)
        mx = sink
        for sb in blocks:
            mx = jnp.maximum(mx, jnp.max(sb, axis=0, keepdims=True))
        es = jnp.exp2(sink - mx)
        es_hi = es.astype(BF16).astype(F32)
        p_tail = jnp.where(p_row == 0, es_hi, jnp.where(p_row == 1, es - es_hi, 0.0)).astype(BF16)
        p_aug = jnp.concatenate([jnp.exp2(sb - mx).astype(BF16) for sb in blocks] + [p_tail], axis=0)
        o = lax.dot_general(p_aug, v_aug, (((0,), (0,)), ((), ())), preferred_element_type=F32)
        o = o[:, :HEAD_DIM] / o[:, HEAD_DIM:]
        for r, hh in enumerate(heads):
            o_ref[rows, hh * HEAD_DIM:(hh + 1) * HEAD_DIM] = o[r * ATTN_BLOCK:(r + 1) * ATTN_BLOCK].astype(BF16)

    units = [(n, g) for n in range(tq // ATTN_BLOCK) for g in range(N_KV_HEADS)]
    biases = {}
    pending = [scores(*u) for u in units[:ATTN_LOOKAHEAD]]
    for k, (n, g) in enumerate(units):
        if k + ATTN_LOOKAHEAD < len(units):
            pending.append(scores(*units[k + ATTN_LOOKAHEAD]))
        if n not in biases:
            biases[n] = band_bias(n)
        finish(n, g, pending.pop(0), *biases[n])


def _attention(q, k, v, attn_sink, seq, layer):
    m = q.shape[0]
    tq = TOKEN_TILE
    bpt = tq // ATTN_BLOCK
    last = m // ATTN_BLOCK - 1
    prev_map = lambda i: (jnp.maximum(i * bpt - 1, 0), 0)
    next_map = lambda i: (jnp.minimum(i * bpt + bpt, last), 0)
    cur_map = lambda i: (i, 0)
    return pl.pallas_call(
        functools.partial(_attn_kernel, layer=layer, seq=seq),
        grid=(m // tq,),
        in_specs=[
            pl.BlockSpec(memory_space=pltpu.SMEM),
            pl.BlockSpec((tq, Q_WIDTH), cur_map),
            pl.BlockSpec((ATTN_BLOCK, KV_WIDTH), prev_map),
            pl.BlockSpec((tq, KV_WIDTH), cur_map),
            pl.BlockSpec((ATTN_BLOCK, KV_WIDTH), next_map),
            pl.BlockSpec((ATTN_BLOCK, KV_WIDTH), prev_map),
            pl.BlockSpec((tq, KV_WIDTH), cur_map),
            pl.BlockSpec((ATTN_BLOCK, KV_WIDTH), next_map),
        ],
        out_specs=pl.BlockSpec((tq, Q_WIDTH), cur_map),
        out_shape=jax.ShapeDtypeStruct((m, Q_WIDTH), BF16),
        compiler_params=_params(("parallel",)),
        name=f"attn_l{layer}_m{m}",
    )(attn_sink, q, k, k, k, v, v, v)


def _merge_kernel(x_ref, mod_ref, ao_ref, so_ref, wgate_hbm, wba_hbm, wbs_hbm, wmo_hbm, lng_ref, lnb_ref, o_ref,
                  h_ref, m_buf, wga_buf, wgb_buf, wba_buf, wbs_buf, wmo_buf, br_sem, mo_sem, *, layer, nj, tn):
    i, n_tiles = pl.program_id(0), pl.num_programs(0)
    t0 = i * nj
    gate = mod_ref[5:6, :]

    def br_copies(chunk, slot):
        col = pl.multiple_of(chunk * tn, tn)
        return (
            pltpu.make_async_copy(wgate_hbm.at[layer, :, pl.ds(col, tn)], wga_buf.at[slot], br_sem.at[slot, 0]),
            pltpu.make_async_copy(wgate_hbm.at[layer, :, pl.ds(D_MODEL + col, tn)], wgb_buf.at[slot],
                                  br_sem.at[slot, 1]),
            pltpu.make_async_copy(wba_hbm.at[layer, :, pl.ds(col, tn)], wba_buf.at[slot], br_sem.at[slot, 2]),
            pltpu.make_async_copy(wbs_hbm.at[layer, :, pl.ds(col, tn)], wbs_buf.at[slot], br_sem.at[slot, 3]),
        )

    def mo_copy(chunk, slot):
        row = pl.multiple_of(chunk * tn, tn)
        return pltpu.make_async_copy(wmo_hbm.at[layer, pl.ds(row, tn), :], wmo_buf.at[slot], mo_sem.at[slot])

    def merged_chunk(slot):
        h = h_ref[...]
        attn = _dot(ao_ref[...], wba_buf[slot])
        sgu = _dot(so_ref[...], wbs_buf[slot])
        merged = _sigmoid(_dot(h, wga_buf[slot])) * attn + _sigmoid(_dot(h, wgb_buf[slot])) * sgu
        return merged.astype(BF16)

    def boundary(j):
        return _stream_boundary(j, t0, nj, i + 1 < n_tiles, br_copies, mo_copy)

    @pl.when(i == 0)
    def _():
        _stream_prime(br_copies, mo_copy)

    slot0 = t0 % 2
    h_ref[...] = _modulate(x_ref, mod_ref, 1)
    o_ref[...] = DEEPNORM_ALPHA * x_ref[...]
    m_buf[slot0] = merged_chunk(slot0)

    def trip(j, carry):
        slot = boundary(j)
        o_ref[...] += gate * _dot(m_buf[slot], wmo_buf[slot])
        m_buf[1 - slot] = merged_chunk(1 - slot)
        return carry

    lax.fori_loop(0, nj - 1, trip, 0)

    slot = boundary(nj - 1)
    r = o_ref[...] + gate * _dot(m_buf[slot], wmo_buf[slot])
    o_ref[...] = _layer_norm(r, lng_ref[...], lnb_ref[...])


def _merge_sublayer(x, mod, row0, seq, attn_o, sgu_o, w_gates, w_br_attn, w_br_sgu, w_mix_out, ln_g, ln_b, layer):
    m = x.shape[0]
    tm, tn = TOKEN_TILE, MERGE_TILE
    nj = D_MODEL // tn
    assert nj >= 2 and m % tm == 0 and seq % tm == 0
    return pl.pallas_call(
        functools.partial(_merge_kernel, layer=layer, nj=nj, tn=tn),
        grid=(m // tm,),
        in_specs=[
            pl.BlockSpec((tm, D_MODEL), lambda i: (i, 0)),
            pl.BlockSpec((None, None, N_MOD, D_MODEL), lambda i: (layer, row0 + (i * tm) // seq, 0, 0)),
            pl.BlockSpec((tm, Q_WIDTH), lambda i: (i, 0)),
            pl.BlockSpec((tm, SGU_WIDTH), lambda i: (i, 0)),
            pl.BlockSpec(memory_space=pl.ANY),
            pl.BlockSpec(memory_space=pl.ANY),
            pl.BlockSpec(memory_space=pl.ANY),
            pl.BlockSpec(memory_space=pl.ANY),
            pl.BlockSpec((None, None, 1, D_MODEL), lambda i: (layer, 1, 0, 0)),
            pl.BlockSpec((None, None, 1, D_MODEL), lambda i: (layer, 1, 0, 0)),
        ],
        out_specs=pl.BlockSpec((tm, D_MODEL), lambda i: (i, 0)),
        out_shape=jax.ShapeDtypeStruct((m, D_MODEL), F32),
        scratch_shapes=[
            pltpu.VMEM((tm, D_MODEL), BF16),
            pltpu.VMEM((2, tm, tn), BF16),
            pltpu.VMEM((2, D_MODEL, tn), BF16),
            pltpu.VMEM((2, D_MODEL, tn), BF16),
            pltpu.VMEM((2, Q_WIDTH, tn), BF16),
            pltpu.VMEM((2, SGU_WIDTH, tn), BF16),
            pltpu.VMEM((2, tn, D_MODEL), BF16),
            pltpu.SemaphoreType.DMA((2, 4)),
            pltpu.SemaphoreType.DMA((2,)),
        ],
        compiler_params=_params(("arbitrary",)),
        name=f"merge_l{layer}_m{m}",
    )(x, mod, attn_o, sgu_o, w_gates, w_br_attn, w_br_sgu, w_mix_out, ln_g, ln_b)


def _rope_tables(seq):
    half = HEAD_DIM // 2
    inv_freq = 1.0 / (ROPE_THETA ** (jnp.arange(half, dtype=F32) / half))
    ang = jnp.arange(seq, dtype=F32)[:, None] * inv_freq[None, :]
    cos = jnp.cos(ang)
    sin = jnp.sin(ang)
    return jnp.concatenate([cos, cos], axis=1), jnp.concatenate([-sin, sin], axis=1)


def kernel(x_prompt, x_sample, c_prompt, c_sample, w_ada, b_ada, ln_g, ln_b, ffn1_w_in, ffn1_w_out, w_mix_in, attn_sink, sgu_ln_g, sgu_ln_b, sgu_w, sgu_b, w_br_attn, w_br_sgu, w_mix_out, ffn2_w_in, ffn2_w_out):
    n_prompt = c_prompt.shape[0]
    c_all = jnp.concatenate([c_prompt, c_sample], axis=0)
    c_all = jnp.pad(c_all, ((0, -c_all.shape[0] % SUBLANE), (0, 0)))
    mod = _ada_mod(c_all, w_ada, b_ada)

    ffn1_in, ffn1_out = ffn1_w_in.astype(BF16), ffn1_w_out.astype(BF16)
    ffn2_in, ffn2_out = ffn2_w_in.astype(BF16), ffn2_w_out.astype(BF16)
    w_qkv = w_mix_in[:, :, :OFF_U].astype(BF16)
    w_usv = w_mix_in[:, :, OFF_U:OFF_GA].astype(BF16)
    w_gates = w_mix_in[:, :, OFF_GA:].astype(BF16)
    w_bra, w_brs, w_mo = w_br_attn.astype(BF16), w_br_sgu.astype(BF16), w_mix_out.astype(BF16)
    sgu_w16 = sgu_w.astype(BF16)
    sgu_b_bc = jnp.broadcast_to(sgu_b[..., None], sgu_b.shape + (LANE,))
    ln_g4 = ln_g.reshape(DEPTH, 3, 1, D_MODEL)
    ln_b4 = ln_b.reshape(DEPTH, 3, 1, D_MODEL)
    sgu_g3 = sgu_ln_g.reshape(DEPTH, 1, SGU_WIDTH)
    sgu_b3 = sgu_ln_b.reshape(DEPTH, 1, SGU_WIDTH)

    def run(x3, row0):
        bsz, seq, _ = x3.shape
        x = x3.reshape(bsz * seq, D_MODEL)
        cos_t, sin_t = _rope_tables(seq)
        for l in range(DEPTH):
            x = _ffn_sublayer(x, mod, row0, seq, ffn1_in, ffn1_out, ln_g4, ln_b4, l, 0)
            q, k, v = _qkv_proj(x, mod, row0, seq, w_qkv, cos_t, sin_t, l)
            sgu_o = _sgu_branch(x, mod, row0, seq, w_usv, sgu_g3, sgu_b3, sgu_w16, sgu_b_bc, l)
            attn_o = _attention(q, k, v, attn_sink, seq, l)
            x = _merge_sublayer(x, mod, row0, seq, attn_o, sgu_o, w_gates, w_bra, w_brs, w_mo, ln_g4, ln_b4, l)
            x = _ffn_sublayer(x, mod, row0, seq, ffn2_in, ffn2_out, ln_g4, ln_b4, l, 2)
        return x.reshape(bsz, seq, D_MODEL)

    return run(x_prompt, 0), run(x_sample, n_prompt)
```

```python
import functools

import jax
import jax.numpy as jnp
from jax import lax
from jax.experimental import pallas as pl
from jax.experimental.pallas import tpu as pltpu

D_MODEL = 2048
DEPTH = 2
HEAD_DIM = 128
N_Q_HEADS = 16
N_KV_HEADS = 4
Q_PER_KV = N_Q_HEADS // N_KV_HEADS
WINDOW = 128
ATTN_BLOCK = 128
ROPE_THETA = 10000.0
SGU_WIDTH = D_MODEL
SGU_CHUNK = 128
SGU_GROUPS = 16
D_FF = 5632
Q_WIDTH = N_Q_HEADS * HEAD_DIM
KV_WIDTH = N_KV_HEADS * HEAD_DIM
OFF_V = Q_WIDTH + KV_WIDTH
OFF_U = OFF_V + KV_WIDTH
OFF_GA = OFF_U + 2 * SGU_WIDTH
N_MOD = 9
DEEPNORM_ALPHA = (2 * DEPTH) ** 0.25
MACARON_WEIGHT = 0.5
LN_EPS = 1e-5
ATTN_SCALE = HEAD_DIM ** -0.5
LOG2E = 1.4426950408889634
Q_PRESCALE = ATTN_SCALE * LOG2E
SGU_LOOKAHEAD = 2
ATTN_LOOKAHEAD = 1
ATTN_TAIL = 16

BF16 = jnp.bfloat16
F32 = jnp.float32

LANE = 128
SUBLANE = 8
TOKEN_TILE = 512
FFN_TOKEN_TILE = 1024
FF_TILE = 512
MERGE_TILE = 512
ADA_TILE = 1024
VMEM_LIMIT = 56 * 1024 * 1024


def _dot(a, b):
    return jnp.dot(a, b, preferred_element_type=F32)


def _sigmoid(x):
    return 1.0 / (1.0 + jnp.exp(-x))


def _gelu_tanh(x):
    return 0.5 * x * (1.0 + jnp.tanh(0.7978845608028654 * (x + 0.044715 * (x * x * x))))


def _layer_norm(r, g, b):
    mu = jnp.mean(r, axis=-1, keepdims=True)
    xc = r - mu
    var = jnp.mean(xc * xc, axis=-1, keepdims=True)
    return xc * lax.rsqrt(var + LN_EPS) * g + b


def _modulate(x_ref, mod_ref, sub):
    shift = mod_ref[3 * sub:3 * sub + 1, :]
    scale = mod_ref[3 * sub + 1:3 * sub + 2, :]
    return (x_ref[...] * (1.0 + scale) + shift).astype(BF16)


def _params(sem):
    return pltpu.CompilerParams(dimension_semantics=sem, vmem_limit_bytes=VMEM_LIMIT)


def _ada_kernel(c_ref, w_ref, b_ref, o_ref):
    c = c_ref[...]
    s = (c * _sigmoid(c)).astype(BF16)
    o_ref[...] = _dot(s, w_ref[...].astype(BF16)) + b_ref[...]


def _ada_mod(c_all, w_ada, b_ada):
    rows = c_all.shape[0]
    n_out = N_MOD * D_MODEL
    out = pl.pallas_call(
        _ada_kernel,
        grid=(DEPTH, n_out // ADA_TILE),
        in_specs=[
            pl.BlockSpec((rows, D_MODEL), lambda l, j: (0, 0)),
            pl.BlockSpec((None, D_MODEL, ADA_TILE), lambda l, j: (l, 0, j)),
            pl.BlockSpec((None, 1, ADA_TILE), lambda l, j: (l, 0, j)),
        ],
        out_specs=pl.BlockSpec((None, rows, ADA_TILE), lambda l, j: (l, 0, j)),
        out_shape=jax.ShapeDtypeStruct((DEPTH, rows, n_out), F32),
        compiler_params=_params(("arbitrary", "arbitrary")),
        name="ada_mod",
    )(c_all, w_ada, b_ada.reshape(DEPTH, 1, n_out))
    return out.reshape(DEPTH, rows, N_MOD, D_MODEL)


def _stream_prime(first_copies, second_copy):
    for c in first_copies(0, 0) + first_copies(1, 1):
        c.start()
    second_copy(0, 0).start()
    for c in first_copies(0, 0):
        c.wait()


def _stream_boundary(j, t0, n_chunks, has_next_tile, first_copies, second_copy):
    slot = (t0 + j) % 2
    second_copy(j, slot).wait()

    @pl.when((j + 1 < n_chunks) | has_next_tile)
    def _():
        for c in first_copies((j + 1) % n_chunks, 1 - slot):
            c.wait()
        second_copy((j + 1) % n_chunks, 1 - slot).start()

    @pl.when((j + 2 < n_chunks) | has_next_tile)
    def _():
        for c in first_copies((j + 2) % n_chunks, slot):
            c.start()
    return slot


def _ffn_kernel(x_hbm, mod_ref, win_hbm, wout_hbm, lng_in_ref, lnb_in_ref, lng_ref, lnb_ref, o_ref,
                x_buf, h_ref, a_buf, wg_buf, wu_buf, wo_buf, x_sem, gu_sem, wo_sem,
                *, layer, sub, nf, tf, ln_in, ln_out):
    i, n_tiles = pl.program_id(0), pl.num_programs(0)
    tm = x_buf.shape[0]
    t0 = i * nf
    res_gate = MACARON_WEIGHT * mod_ref[3 * sub + 2:3 * sub + 3, :]
    halves = [slice(r * (tm // 2), (r + 1) * (tm // 2)) for r in range(2)]

    def x_copy(tile):
        return pltpu.make_async_copy(x_hbm.at[pl.ds(tile * tm, tm), :], x_buf, x_sem)

    def gu_copies(chunk, slot):
        col = pl.multiple_of(chunk * tf, tf)
        return (pltpu.make_async_copy(win_hbm.at[layer, :, pl.ds(col, tf)], wg_buf.at[slot], gu_sem.at[slot, 0]),
                pltpu.make_async_copy(win_hbm.at[layer, :, pl.ds(D_FF + col, tf)], wu_buf.at[slot],
                                      gu_sem.at[slot, 1]))

    def wo_copy(chunk, slot):
        row = pl.multiple_of(chunk * tf, tf)
        return pltpu.make_async_copy(wout_hbm.at[layer, pl.ds(row, tf), :], wo_buf.at[slot], wo_sem.at[slot])

    def gate_up(h, slot):
        g = _dot(h, wg_buf[slot])
        u = _dot(h, wu_buf[slot])
        return (g * _sigmoid(g) * u).astype(BF16)

    def boundary(j):
        return _stream_boundary(j, t0, nf, i + 1 < n_tiles, gu_copies, wo_copy)

    @pl.when(i == 0)
    def _():
        x_copy(0).start()
        _stream_prime(gu_copies, wo_copy)

    x_copy(i).wait()
    slot0 = t0 % 2
    shift = mod_ref[3 * sub:3 * sub + 1, :]
    scale = mod_ref[3 * sub + 1:3 * sub + 2, :]
    for rows in halves:
        x = x_buf[rows, :]
        if ln_in:
            x = _layer_norm(x, lng_in_ref[...], lnb_in_ref[...])
        h = (x * (1.0 + scale) + shift).astype(BF16)
        h_ref[rows, :] = h
        o_ref[rows, :] = DEEPNORM_ALPHA * x
        a_buf[slot0, rows, :] = gate_up(h, slot0)

    @pl.when(i + 1 < n_tiles)
    def _():
        x_copy(i + 1).start()

    def trip(j, carry):
        slot = boundary(j)
        o_ref[...] += res_gate * _dot(a_buf[slot], wo_buf[slot])
        a_buf[1 - slot] = gate_up(h_ref[...], 1 - slot)
        return carry

    lax.fori_loop(0, nf - 1, trip, 0)

    slot = boundary(nf - 1)
    res = [o_ref[rows, :] + res_gate * _dot(a_buf[slot, rows, :], wo_buf[slot]) for rows in halves]
    for rows, r in zip(halves, res):
        o_ref[rows, :] = _layer_norm(r, lng_ref[...], lnb_ref[...]) if ln_out else r


def _ffn_sublayer(x, mod, row0, seq, w_in, w_out, ln_g, ln_b, layer, sub, ln_in, ln_out):
    m = x.shape[0]
    tm, tf = FFN_TOKEN_TILE, FF_TILE
    nf = D_FF // tf
    assert nf >= 2 and m % tm == 0 and seq % tm == 0
    ln_src = ln_in if ln_in is not None else (layer, sub)
    return pl.pallas_call(
        functools.partial(_ffn_kernel, layer=layer, sub=sub, nf=nf, tf=tf, ln_in=ln_in is not None, ln_out=ln_out),
        grid=(m // tm,),
        in_specs=[
            pl.BlockSpec(memory_space=pl.ANY),
            pl.BlockSpec((None, None, N_MOD, D_MODEL), lambda i: (layer, row0 + (i * tm) // seq, 0, 0)),
            pl.BlockSpec(memory_space=pl.ANY),
            pl.BlockSpec(memory_space=pl.ANY),
            pl.BlockSpec((None, None, 1, D_MODEL), lambda i: (*ln_src, 0, 0)),
            pl.BlockSpec((None, None, 1, D_MODEL), lambda i: (*ln_src, 0, 0)),
            pl.BlockSpec((None, None, 1, D_MODEL), lambda i: (layer, sub, 0, 0)),
            pl.BlockSpec((None, None, 1, D_MODEL), lambda i: (layer, sub, 0, 0)),
        ],
        out_specs=pl.BlockSpec((tm, D_MODEL), lambda i: (i, 0)),
        out_shape=jax.ShapeDtypeStruct((m, D_MODEL), F32),
        scratch_shapes=[
            pltpu.VMEM((tm, D_MODEL), F32),
            pltpu.VMEM((tm, D_MODEL), BF16),
            pltpu.VMEM((2, tm, tf), BF16),
            pltpu.VMEM((2, D_MODEL, tf), BF16),
            pltpu.VMEM((2, D_MODEL, tf), BF16),
            pltpu.VMEM((2, tf, D_MODEL), BF16),
            pltpu.SemaphoreType.DMA(()),
            pltpu.SemaphoreType.DMA((2, 2)),
            pltpu.SemaphoreType.DMA((2,)),
        ],
        compiler_params=_params(("arbitrary",)),
        name=f"ffn_l{layer}_s{sub}_m{m}",
    )(x, mod, w_in, w_out, ln_g, ln_b, ln_g, ln_b)


def _qkv_kernel(x_ref, mod_ref, w_ref, cos_ref, sin_ref, q_ref, k_ref, v_ref):
    h = _modulate(x_ref, mod_ref, 1)
    cos = cos_ref[...]
    sin = sin_ref[...]
    chunk = 4 * HEAD_DIM
    for c in range((Q_WIDTH + 2 * KV_WIDTH) // chunk):
        z = _dot(h, w_ref[:, c * chunk:(c + 1) * chunk])
        if c * chunk >= OFF_V:
            v_ref[...] = z.astype(BF16)
            continue
        for t in range(4):
            zh = z[:, t * HEAD_DIM:(t + 1) * HEAD_DIM]
            r = zh * cos + pltpu.roll(zh, HEAD_DIM // 2, 1) * sin
            if c * chunk < Q_WIDTH:
                q_ref[:, c * chunk + t * HEAD_DIM:c * chunk + (t + 1) * HEAD_DIM] = (r * Q_PRESCALE).astype(BF16)
            else:
                k_ref[:, t * HEAD_DIM:(t + 1) * HEAD_DIM] = r.astype(BF16)


def _qkv_proj(x, mod, row0, seq, w_qkv, cos_t, sin_t, layer):
    m = x.shape[0]
    tm = TOKEN_TILE
    n = Q_WIDTH + 2 * KV_WIDTH
    return pl.pallas_call(
        _qkv_kernel,
        grid=(m // tm,),
        in_specs=[
            pl.BlockSpec((tm, D_MODEL), lambda i: (i, 0)),
            pl.BlockSpec((None, None, N_MOD, D_MODEL), lambda i: (layer, row0 + (i * tm) // seq, 0, 0)),
            pl.BlockSpec((None, D_MODEL, n), lambda i: (layer, 0, 0), pipeline_mode=pl.Buffered(1)),
            pl.BlockSpec((tm, HEAD_DIM), lambda i: (i % (seq // tm), 0)),
            pl.BlockSpec((tm, HEAD_DIM), lambda i: (i % (seq // tm), 0)),
        ],
        out_specs=[
            pl.BlockSpec((tm, Q_WIDTH), lambda i: (i, 0)),
            pl.BlockSpec((tm, KV_WIDTH), lambda i: (i, 0)),
            pl.BlockSpec((tm, KV_WIDTH), lambda i: (i, 0)),
        ],
        out_shape=[
            jax.ShapeDtypeStruct((m, Q_WIDTH), BF16),
            jax.ShapeDtypeStruct((m, KV_WIDTH), BF16),
            jax.ShapeDtypeStruct((m, KV_WIDTH), BF16),
        ],
        compiler_params=_params(("parallel",)),
        name=f"qkv_l{layer}_m{m}",
    )(x, mod, w_qkv, cos_t, sin_t)


def _sgu_kernel(x_ref, mod_ref, w_ref, lng_ref, lnb_ref, ws_ref, bs_ref, o_ref, vn_ref):
    tm = x_ref.shape[0]
    h = _modulate(x_ref, mod_ref, 1)
    nchunk = tm // SGU_CHUNK
    gpc = 4

    def u_chunk(c):
        return _gelu_tanh(_dot(h, w_ref[:, c * gpc * LANE:(c + 1) * gpc * LANE]))

    sv = _gelu_tanh(_dot(h, w_ref[:, SGU_WIDTH:2 * SGU_WIDTH]))
    pending = [u_chunk(c) for c in range(SGU_LOOKAHEAD)]
    vn_ref[...] = _layer_norm(sv, lng_ref[...], lnb_ref[...]).astype(BF16)
    for c in range(SGU_GROUPS // gpc):
        if c + SGU_LOOKAHEAD < SGU_GROUPS // gpc:
            pending.append(u_chunk(c + SGU_LOOKAHEAD))
        u = pending.pop(0)
        for t in range(gpc):
            g = c * gpc + t
            col = slice(g * LANE, (g + 1) * LANE)
            rhs = jnp.concatenate([vn_ref[n * SGU_CHUNK:(n + 1) * SGU_CHUNK, col] for n in range(nchunk)], axis=1)
            z = _dot(ws_ref[g], rhs)
            for n in range(nchunk):
                rows = slice(n * SGU_CHUNK, (n + 1) * SGU_CHUNK)
                zz = z[:, n * LANE:(n + 1) * LANE] + bs_ref[g]
                o_ref[rows, col] = (u[rows, t * LANE:(t + 1) * LANE] * zz).astype(BF16)


def _sgu_branch(x, mod, row0, seq, w_usv, sgu_ln_g, sgu_ln_b, sgu_w, sgu_b_bc, layer):
    m = x.shape[0]
    tm = TOKEN_TILE
    return pl.pallas_call(
        _sgu_kernel,
        grid=(m // tm,),
        in_specs=[
            pl.BlockSpec((tm, D_MODEL), lambda i: (i, 0)),
            pl.BlockSpec((None, None, N_MOD, D_MODEL), lambda i: (layer, row0 + (i * tm) // seq, 0, 0)),
            pl.BlockSpec((None, D_MODEL, 2 * SGU_WIDTH), lambda i: (layer, 0, 0), pipeline_mode=pl.Buffered(1)),
            pl.BlockSpec((None, 1, SGU_WIDTH), lambda i: (layer, 0, 0)),
            pl.BlockSpec((None, 1, SGU_WIDTH), lambda i: (layer, 0, 0)),
            pl.BlockSpec((None, SGU_GROUPS, SGU_CHUNK, SGU_CHUNK), lambda i: (layer, 0, 0, 0)),
            pl.BlockSpec((None, SGU_GROUPS, SGU_CHUNK, LANE), lambda i: (layer, 0, 0, 0)),
        ],
        out_specs=pl.BlockSpec((tm, SGU_WIDTH), lambda i: (i, 0)),
        out_shape=jax.ShapeDtypeStruct((m, SGU_WIDTH), BF16),
        scratch_shapes=[pltpu.VMEM((tm, SGU_WIDTH), BF16)],
        compiler_params=_params(("parallel",)),
        name=f"sgu_l{layer}_m{m}",
    )(x, mod, w_usv, sgu_ln_g, sgu_ln_b, sgu_w, sgu_b_bc)


def _attn_kernel(sink_ref, q_ref, kp_ref, kc_ref, kn_ref, vp_ref, vc_ref, vn_ref, o_ref, *, layer, seq):
    tq = q_ref.shape[0]
    pos0 = (pl.program_id(0) * tq) % seq
    kall = jnp.concatenate([kp_ref[...], kc_ref[...], kn_ref[...]], axis=0)
    vall = jnp.concatenate([vp_ref[...], vc_ref[...], vn_ref[...]], axis=0)
    span = 3 * ATTN_BLOCK
    kj = lax.broadcasted_iota(jnp.int32, (span, ATTN_BLOCK), 0)
    qi = lax.broadcasted_iota(jnp.int32, (span, ATTN_BLOCK), 1)
    band = jnp.abs(kj - ATTN_BLOCK - qi) <= WINDOW
    tail_row = lax.broadcasted_iota(jnp.int32, (ATTN_TAIL, 2 * HEAD_DIM), 0)
    tail_col = lax.broadcasted_iota(jnp.int32, (ATTN_TAIL, 2 * HEAD_DIM), 1)
    v_tail = jnp.where((tail_row < 2) & (tail_col >= HEAD_DIM), 1.0, 0.0).astype(BF16)
    ones_blk = jnp.ones((span, HEAD_DIM), BF16)
    p_row = lax.broadcasted_iota(jnp.int32, (ATTN_TAIL, Q_PER_KV * ATTN_BLOCK), 0)
    def band_bias(n):
        kpos = pos0 + (n - 1) * ATTN_BLOCK + kj
        ok = band & (kpos >= 0) & (kpos < seq)
        bias = jnp.where(ok, 0.0, -jnp.inf)
        bias = jnp.concatenate([bias] * Q_PER_KV, axis=1)
        return bias[:ATTN_BLOCK], bias[2 * ATTN_BLOCK:]

    def scores(n, g):
        rows = slice(n * ATTN_BLOCK, (n + 1) * ATTN_BLOCK)
        kg = kall[n * ATTN_BLOCK:n * ATTN_BLOCK + span, g * HEAD_DIM:(g + 1) * HEAD_DIM]
        heads = [g * Q_PER_KV + r for r in range(Q_PER_KV)]
        qs = jnp.concatenate([q_ref[rows, hh * HEAD_DIM:(hh + 1) * HEAD_DIM] for hh in heads], axis=0)
        return lax.dot_general(kg, qs, (((1,), (1,)), ((), ())), preferred_element_type=F32)

    def finish(n, g, s, bias_prev, bias_next):
        rows = slice(n * ATTN_BLOCK, (n + 1) * ATTN_BLOCK)
        vg = vall[n * ATTN_BLOCK:n * ATTN_BLOCK + span, g * HEAD_DIM:(g + 1) * HEAD_DIM]
        v_aug = jnp.concatenate([jnp.concatenate([vg, ones_blk], axis=1), v_tail], axis=0)
        heads = [g * Q_PER_KV + r for r in range(Q_PER_KV)]
        blocks = [s[:ATTN_BLOCK] + bias_prev, s[ATTN_BLOCK:2 * ATTN_BLOCK], s[2 * ATTN_BLOCK:] + bias_next]
        sink = jnp.concatenate(
            [jnp.full((1, ATTN_BLOCK), sink_ref[layer, hh] * LOG2E, F32) for hh in heads], axis=1)
        mx = sink
        for sb in blocks:
            mx = jnp.maximum(mx, jnp.max(sb, axis=0, keepdims=True))
        es = jnp.exp2(sink - mx)
        es_hi = es.astype(BF16).astype(F32)
        p_tail = jnp.where(p_row == 0, es_hi, jnp.where(p_row == 1, es - es_hi, 0.0)).astype(BF16)
        p_aug = jnp.concatenate([jnp.exp2(sb - mx).astype(BF16) for sb in blocks] + [p_tail], axis=0)
        o = lax.dot_general(p_aug, v_aug, (((0,), (0,)), ((), ())), preferred_element_type=F32)
        o = o[:, :HEAD_DIM] / o[:, HEAD_DIM:]
        for r, hh in enumerate(heads):
            o_ref[rows, hh * HEAD_DIM:(hh + 1) * HEAD_DIM] = o[r * ATTN_BLOCK:(r + 1) * ATTN_BLOCK].astype(BF16)

    units = [(n, g) for n in range(tq // ATTN_BLOCK) for g in range(N_KV_HEADS)]
    biases = {}
    pending = [scores(*u) for u in units[:ATTN_LOOKAHEAD]]
    for k, (n, g) in enumerate(units):
        if k + ATTN_LOOKAHEAD < len(units):
            pending.append(scores(*units[k + ATTN_LOOKAHEAD]))
        if n not in biases:
            biases[n] = band_bias(n)
        finish(n, g, pending.pop(0), *biases[n])


def _attention(q, k, v, attn_sink, seq, layer):
    m = q.shape[0]
    tq = TOKEN_TILE
    bpt = tq // ATTN_BLOCK
    last = m // ATTN_BLOCK - 1
    prev_map = lambda i: (jnp.maximum(i * bpt - 1, 0), 0)
    next_map = lambda i: (jnp.minimum(i * bpt + bpt, last), 0)
    cur_map = lambda i: (i, 0)
    return pl.pallas_call(
        functools.partial(_attn_kernel, layer=layer, seq=seq),
        grid=(m // tq,),
        in_specs=[
            pl.BlockSpec(memory_space=pltpu.SMEM),
            pl.BlockSpec((tq, Q_WIDTH), cur_map),
            pl.BlockSpec((ATTN_BLOCK, KV_WIDTH), prev_map),
            pl.BlockSpec((tq, KV_WIDTH), cur_map),
            pl.BlockSpec((ATTN_BLOCK, KV_WIDTH), next_map),
            pl.BlockSpec((ATTN_BLOCK, KV_WIDTH), prev_map),
            pl.BlockSpec((tq, KV_WIDTH), cur_map),
            pl.BlockSpec((ATTN_BLOCK, KV_WIDTH), next_map),
        ],
        out_specs=pl.BlockSpec((tq, Q_WIDTH), cur_map),
        out_shape=jax.ShapeDtypeStruct((m, Q_WIDTH), BF16),
        compiler_params=_params(("parallel",)),
        name=f"attn_l{layer}_m{m}",
    )(attn_sink, q, k, k, k, v, v, v)


def _merge_kernel(x_ref, mod_ref, ao_ref, so_ref, wgate_hbm, wba_hbm, wbs_hbm, wmo_hbm, o_ref,
                  h_ref, m_buf, wga_buf, wgb_buf, wba_buf, wbs_buf, wmo_buf, br_sem, mo_sem, *, layer, nj, tn):
    i, n_tiles = pl.program_id(0), pl.num_programs(0)
    t0 = i * nj
    gate = mod_ref[5:6, :]

    def br_copies(chunk, slot):
        col = pl.multiple_of(chunk * tn, tn)
        return (
            pltpu.make_async_copy(wgate_hbm.at[layer, :, pl.ds(col, tn)], wga_buf.at[slot], br_sem.at[slot, 0]),
            pltpu.make_async_copy(wgate_hbm.at[layer, :, pl.ds(D_MODEL + col, tn)], wgb_buf.at[slot],
                                  br_sem.at[slot, 1]),
            pltpu.make_async_copy(wba_hbm.at[layer, :, pl.ds(col, tn)], wba_buf.at[slot], br_sem.at[slot, 2]),
            pltpu.make_async_copy(wbs_hbm.at[layer, :, pl.ds(col, tn)], wbs_buf.at[slot], br_sem.at[slot, 3]),
        )

    def mo_copy(chunk, slot):
        row = pl.multiple_of(chunk * tn, tn)
        return pltpu.make_async_copy(wmo_hbm.at[layer, pl.ds(row, tn), :], wmo_buf.at[slot], mo_sem.at[slot])

    def merged_chunk(slot):
        h = h_ref[...]
        gate_a = _sigmoid(_dot(h, wga_buf[slot]))
        gate_b = _sigmoid(_dot(h, wgb_buf[slot]))
        attn = _dot(ao_ref[...], wba_buf[slot])
        sgu = _dot(so_ref[...], wbs_buf[slot])
        return (gate_a * attn + gate_b * sgu).astype(BF16)

    def boundary(j):
        return _stream_boundary(j, t0, nj, i + 1 < n_tiles, br_copies, mo_copy)

    @pl.when(i == 0)
    def _():
        _stream_prime(br_copies, mo_copy)

    slot0 = t0 % 2
    h_ref[...] = _modulate(x_ref, mod_ref, 1)
    o_ref[...] = DEEPNORM_ALPHA * x_ref[...]
    m_buf[slot0] = merged_chunk(slot0)

    def trip(j, carry):
        slot = boundary(j)
        o_ref[...] += gate * _dot(m_buf[slot], wmo_buf[slot])
        m_buf[1 - slot] = merged_chunk(1 - slot)
        return carry

    lax.fori_loop(0, nj - 1, trip, 0)

    slot = boundary(nj - 1)
    o_ref[...] += gate * _dot(m_buf[slot], wmo_buf[slot])


def _merge_sublayer(x, mod, row0, seq, attn_o, sgu_o, w_gates, w_br_attn, w_br_sgu, w_mix_out, layer):
    m = x.shape[0]
    tm, tn = TOKEN_TILE, MERGE_TILE
    nj = D_MODEL // tn
    assert nj >= 2 and m % tm == 0 and seq % tm == 0
    return pl.pallas_call(
        functools.partial(_merge_kernel, layer=layer, nj=nj, tn=tn),
        grid=(m // tm,),
        in_specs=[
            pl.BlockSpec((tm, D_MODEL), lambda i: (i, 0)),
            pl.BlockSpec((None, None, N_MOD, D_MODEL), lambda i: (layer, row0 + (i * tm) // seq, 0, 0)),
            pl.BlockSpec((tm, Q_WIDTH), lambda i: (i, 0)),
            pl.BlockSpec((tm, SGU_WIDTH), lambda i: (i, 0)),
            pl.BlockSpec(memory_space=pl.ANY),
            pl.BlockSpec(memory_space=pl.ANY),
            pl.BlockSpec(memory_space=pl.ANY),
            pl.BlockSpec(memory_space=pl.ANY),
        ],
        out_specs=pl.BlockSpec((tm, D_MODEL), lambda i: (i, 0)),
        out_shape=jax.ShapeDtypeStruct((m, D_MODEL), F32),
        scratch_shapes=[
            pltpu.VMEM((tm, D_MODEL), BF16),
            pltpu.VMEM((2, tm, tn), BF16),
            pltpu.VMEM((2, D_MODEL, tn), BF16),
            pltpu.VMEM((2, D_MODEL, tn), BF16),
            pltpu.VMEM((2, Q_WIDTH, tn), BF16),
            pltpu.VMEM((2, SGU_WIDTH, tn), BF16),
            pltpu.VMEM((2, tn, D_MODEL), BF16),
            pltpu.SemaphoreType.DMA((2, 4)),
            pltpu.SemaphoreType.DMA((2,)),
        ],
        compiler_params=_params(("arbitrary",)),
        name=f"merge_l{layer}_m{m}",
    )(x, mod, attn_o, sgu_o, w_gates, w_br_attn, w_br_sgu, w_mix_out)


def _rope_tables(seq):
    half = HEAD_DIM // 2
    inv_freq = 1.0 / (ROPE_THETA ** (jnp.arange(half, dtype=F32) / half))
    ang = jnp.arange(seq, dtype=F32)[:, None] * inv_freq[None, :]
    cos = jnp.cos(ang)
    sin = jnp.sin(ang)
    return jnp.concatenate([cos, cos], axis=1), jnp.concatenate([-sin, sin], axis=1)


def kernel(x_prompt, x_sample, c_prompt, c_sample, w_ada, b_ada, ln_g, ln_b, ffn1_w_in, ffn1_w_out, w_mix_in, attn_sink, sgu_ln_g, sgu_ln_b, sgu_w, sgu_b, w_br_attn, w_br_sgu, w_mix_out, ffn2_w_in, ffn2_w_out):
    n_prompt = c_prompt.shape[0]
    c_all = jnp.concatenate([c_prompt, c_sample], axis=0)
    c_all = jnp.pad(c_all, ((0, -c_all.shape[0] % SUBLANE), (0, 0)))
    mod = _ada_mod(c_all, w_ada, b_ada)

    ffn1_in, ffn1_out = ffn1_w_in.astype(BF16), ffn1_w_out.astype(BF16)
    ffn2_in, ffn2_out = ffn2_w_in.astype(BF16), ffn2_w_out.astype(BF16)
    w_qkv = w_mix_in[:, :, :OFF_U].astype(BF16)
    w_usv = w_mix_in[:, :, OFF_U:OFF_GA].astype(BF16)
    w_gates = w_mix_in[:, :, OFF_GA:].astype(BF16)
    w_bra, w_brs, w_mo = w_br_attn.astype(BF16), w_br_sgu.astype(BF16), w_mix_out.astype(BF16)
    sgu_w16 = sgu_w.astype(BF16)
    sgu_b_bc = jnp.broadcast_to(sgu_b[..., None], sgu_b.shape + (LANE,))
    ln_g4 = ln_g.reshape(DEPTH, 3, 1, D_MODEL)
    ln_b4 = ln_b.reshape(DEPTH, 3, 1, D_MODEL)
    sgu_g3 = sgu_ln_g.reshape(DEPTH, 1, SGU_WIDTH)
    sgu_b3 = sgu_ln_b.reshape(DEPTH, 1, SGU_WIDTH)

    def run(x3, row0):
        bsz, seq, _ = x3.shape
        x = x3.reshape(bsz * seq, D_MODEL)
        cos_t, sin_t = _rope_tables(seq)
        for l in range(DEPTH):
            x = _ffn_sublayer(x, mod, row0, seq, ffn1_in, ffn1_out, ln_g4, ln_b4, l, 0,
                              ln_in=(l - 1, 2) if l > 0 else None, ln_out=True)
            q, k, v = _qkv_proj(x, mod, row0, seq, w_qkv, cos_t, sin_t, l)
            sgu_o = _sgu_branch(x, mod, row0, seq, w_usv, sgu_g3, sgu_b3, sgu_w16, sgu_b_bc, l)
            attn_o = _attention(q, k, v, attn_sink, seq, l)
            x = _merge_sublayer(x, mod, row0, seq, attn_o, sgu_o, w_gates, w_bra, w_brs, w_mo, l)
            x = _ffn_sublayer(x, mod, row0, seq, ffn2_in, ffn2_out, ln_g4, ln_b4, l, 2,
                              ln_in=(l, 1), ln_out=l == DEPTH - 1)
        return x.reshape(bsz, seq, D_MODEL)

    return run(x_prompt, 0), run(x_sample, n_prompt)
```

```python
import functools

import jax
import jax.numpy as jnp
from jax import lax
from jax.experimental import pallas as pl
from jax.experimental.pallas import tpu as pltpu

D_MODEL = 2048
DEPTH = 2
HEAD_DIM = 128
N_Q_HEADS = 16
N_KV_HEADS = 4
Q_PER_KV = N_Q_HEADS // N_KV_HEADS
WINDOW = 128
ATTN_BLOCK = 128
ROPE_THETA = 10000.0
SGU_WIDTH = D_MODEL
SGU_CHUNK = 128
SGU_GROUPS = 16
D_FF = 5632
Q_WIDTH = N_Q_HEADS * HEAD_DIM
KV_WIDTH = N_KV_HEADS * HEAD_DIM
OFF_V = Q_WIDTH + KV_WIDTH
OFF_U = OFF_V + KV_WIDTH
OFF_GA = OFF_U + 2 * SGU_WIDTH
N_MOD = 9
DEEPNORM_ALPHA = (2 * DEPTH) ** 0.25
MACARON_WEIGHT = 0.5
LN_EPS = 1e-5
ATTN_SCALE = HEAD_DIM ** -0.5
LOG2E = 1.4426950408889634
Q_PRESCALE = ATTN_SCALE * LOG2E
SGU_ROW_PARTS = 4
SGU_LOOKAHEAD = 2
ATTN_LOOKAHEAD = 1
ATTN_TAIL = 16

BF16 = jnp.bfloat16
F32 = jnp.float32

LANE = 128
SUBLANE = 8
TOKEN_TILE = 512
FFN_TOKEN_TILE = 1024
FFN_ROW_PARTS = 4
FF_TILE = 512
MERGE_TILE = 512
MERGE_ROW_PARTS = 2
ADA_TILE = 1024
VMEM_LIMIT = 56 * 1024 * 1024


def _dot(a, b):
    return jnp.dot(a, b, preferred_element_type=F32)


def _sigmoid(x):
    return 1.0 / (1.0 + jnp.exp(-x))


def _gelu_tanh(x):
    return 0.5 * x * (1.0 + jnp.tanh(0.7978845608028654 * (x + 0.044715 * (x * x * x))))


def _layer_norm(r, g, b):
    mu = jnp.mean(r, axis=-1, keepdims=True)
    xc = r - mu
    var = jnp.mean(xc * xc, axis=-1, keepdims=True)
    return xc * lax.rsqrt(var + LN_EPS) * g + b


def _modulate(x_ref, mod_ref, sub):
    shift = mod_ref[3 * sub:3 * sub + 1, :]
    scale = mod_ref[3 * sub + 1:3 * sub + 2, :]
    return (x_ref[...] * (1.0 + scale) + shift).astype(BF16)


def _params(sem):
    return pltpu.CompilerParams(dimension_semantics=sem, vmem_limit_bytes=VMEM_LIMIT)


def _ada_kernel(c_ref, w_ref, b_ref, o_ref):
    c = c_ref[...]
    s = (c * _sigmoid(c)).astype(BF16)
    o_ref[...] = _dot(s, w_ref[...].astype(BF16)) + b_ref[...]


def _ada_mod(c_all, w_ada, b_ada):
    rows = c_all.shape[0]
    n_out = N_MOD * D_MODEL
    out = pl.pallas_call(
        _ada_kernel,
        grid=(DEPTH, n_out // ADA_TILE),
        in_specs=[
            pl.BlockSpec((rows, D_MODEL), lambda l, j: (0, 0)),
            pl.BlockSpec((None, D_MODEL, ADA_TILE), lambda l, j: (l, 0, j)),
            pl.BlockSpec((None, 1, ADA_TILE), lambda l, j: (l, 0, j)),
        ],
        out_specs=pl.BlockSpec((None, rows, ADA_TILE), lambda l, j: (l, 0, j)),
        out_shape=jax.ShapeDtypeStruct((DEPTH, rows, n_out), F32),
        compiler_params=_params(("arbitrary", "arbitrary")),
        name="ada_mod",
    )(c_all, w_ada, b_ada.reshape(DEPTH, 1, n_out))
    return out.reshape(DEPTH, rows, N_MOD, D_MODEL)


def _stream_prime(first_copies, second_copy):
    for c in first_copies(0, 0) + first_copies(1, 1):
        c.start()
    second_copy(0, 0).start()
    for c in first_copies(0, 0):
        c.wait()


def _stream_boundary(j, t0, n_chunks, has_next_tile, first_copies, second_copy):
    slot = (t0 + j) % 2
    second_copy(j, slot).wait()

    @pl.when((j + 1 < n_chunks) | has_next_tile)
    def _():
        for c in first_copies((j + 1) % n_chunks, 1 - slot):
            c.wait()
        second_copy((j + 1) % n_chunks, 1 - slot).start()

    @pl.when((j + 2 < n_chunks) | has_next_tile)
    def _():
        for c in first_copies((j + 2) % n_chunks, slot):
            c.start()
    return slot


def _ffn_kernel(x_hbm, mod_ref, win_hbm, wout_hbm, lng_in_ref, lnb_in_ref, lng_ref, lnb_ref, o_ref,
                x_buf, h_ref, a_buf, wg_buf, wu_buf, wo_buf, x_sem, gu_sem, wo_sem,
                *, layer, sub, nf, tf, ln_in, ln_out):
    i, n_tiles = pl.program_id(0), pl.num_programs(0)
    tm = x_buf.shape[0]
    t0 = i * nf
    res_gate = MACARON_WEIGHT * mod_ref[3 * sub + 2:3 * sub + 3, :]
    part = tm // FFN_ROW_PARTS
    row_parts = [slice(r * part, (r + 1) * part) for r in range(FFN_ROW_PARTS)]

    def x_copy(tile):
        return pltpu.make_async_copy(x_hbm.at[pl.ds(tile * tm, tm), :], x_buf, x_sem)

    def gu_copies(chunk, slot):
        col = pl.multiple_of(chunk * tf, tf)
        return (pltpu.make_async_copy(win_hbm.at[layer, :, pl.ds(col, tf)], wg_buf.at[slot], gu_sem.at[slot, 0]),
                pltpu.make_async_copy(win_hbm.at[layer, :, pl.ds(D_FF + col, tf)], wu_buf.at[slot],
                                      gu_sem.at[slot, 1]))

    def wo_copy(chunk, slot):
        row = pl.multiple_of(chunk * tf, tf)
        return pltpu.make_async_copy(wout_hbm.at[layer, pl.ds(row, tf), :], wo_buf.at[slot], wo_sem.at[slot])

    def gate_up(h, slot):
        g = _dot(h, wg_buf[slot])
        u = _dot(h, wu_buf[slot])
        return (g * _sigmoid(g) * u).astype(BF16)

    def boundary(j):
        return _stream_boundary(j, t0, nf, i + 1 < n_tiles, gu_copies, wo_copy)

    @pl.when(i == 0)
    def _():
        x_copy(0).start()
        _stream_prime(gu_copies, wo_copy)

    x_copy(i).wait()
    slot0 = t0 % 2
    shift = mod_ref[3 * sub:3 * sub + 1, :]
    scale = mod_ref[3 * sub + 1:3 * sub + 2, :]
    for rows in row_parts:
        x = x_buf[rows, :]
        if ln_in:
            x = _layer_norm(x, lng_in_ref[...], lnb_in_ref[...])
        h = (x * (1.0 + scale) + shift).astype(BF16)
        h_ref[rows, :] = h
        o_ref[rows, :] = DEEPNORM_ALPHA * x
        a_buf[slot0, rows, :] = gate_up(h, slot0)

    @pl.when(i + 1 < n_tiles)
    def _():
        x_copy(i + 1).start()

    def trip(j, carry):
        slot = boundary(j)
        o_ref[...] += res_gate * _dot(a_buf[slot], wo_buf[slot])
        a_buf[1 - slot] = gate_up(h_ref[...], 1 - slot)
        return carry

    lax.fori_loop(0, nf - 1, trip, 0)

    slot = boundary(nf - 1)
    res = [o_ref[rows, :] + res_gate * _dot(a_buf[slot, rows, :], wo_buf[slot]) for rows in row_parts]
    for rows, r in zip(row_parts, res):
        o_ref[rows, :] = _layer_norm(r, lng_ref[...], lnb_ref[...]) if ln_out else r


def _ffn_sublayer(x, mod, row0, seq, w_in, w_out, ln_g, ln_b, layer, sub, ln_in, ln_out):
    m = x.shape[0]
    tm, tf = FFN_TOKEN_TILE, FF_TILE
    nf = D_FF // tf
    assert nf >= 2 and m % tm == 0 and seq % tm == 0
    ln_src = ln_in if ln_in is not None else (layer, sub)
    return pl.pallas_call(
        functools.partial(_ffn_kernel, layer=layer, sub=sub, nf=nf, tf=tf, ln_in=ln_in is not None, ln_out=ln_out),
        grid=(m // tm,),
        in_specs=[
            pl.BlockSpec(memory_space=pl.ANY),
            pl.BlockSpec((None, None, N_MOD, D_MODEL), lambda i: (layer, row0 + (i * tm) // seq, 0, 0)),
            pl.BlockSpec(memory_space=pl.ANY),
            pl.BlockSpec(memory_space=pl.ANY),
            pl.BlockSpec((None, None, 1, D_MODEL), lambda i: (*ln_src, 0, 0)),
            pl.BlockSpec((None, None, 1, D_MODEL), lambda i: (*ln_src, 0, 0)),
            pl.BlockSpec((None, None, 1, D_MODEL), lambda i: (layer, sub, 0, 0)),
            pl.BlockSpec((None, None, 1, D_MODEL), lambda i: (layer, sub, 0, 0)),
        ],
        out_specs=pl.BlockSpec((tm, D_MODEL), lambda i: (i, 0)),
        out_shape=jax.ShapeDtypeStruct((m, D_MODEL), F32),
        scratch_shapes=[
            pltpu.VMEM((tm, D_MODEL), F32),
            pltpu.VMEM((tm, D_MODEL), BF16),
            pltpu.VMEM((2, tm, tf), BF16),
            pltpu.VMEM((2, D_MODEL, tf), BF16),
            pltpu.VMEM((2, D_MODEL, tf), BF16),
            pltpu.VMEM((2, tf, D_MODEL), BF16),
            pltpu.SemaphoreType.DMA(()),
            pltpu.SemaphoreType.DMA((2, 2)),
            pltpu.SemaphoreType.DMA((2,)),
        ],
        compiler_params=_params(("arbitrary",)),
        name=f"ffn_l{layer}_s{sub}_m{m}",
    )(x, mod, w_in, w_out, ln_g, ln_b, ln_g, ln_b)


def _qkv_kernel(x_ref, mod_ref, w_ref, cos_ref, sin_ref, q_ref, k_ref, v_ref):
    h = _modulate(x_ref, mod_ref, 1)
    cos = cos_ref[...]
    sin = sin_ref[...]
    chunk = 4 * HEAD_DIM
    for c in range((Q_WIDTH + 2 * KV_WIDTH) // chunk):
        z = _dot(h, w_ref[:, c * chunk:(c + 1) * chunk])
        if c * chunk >= OFF_V:
            v_ref[...] = z.astype(BF16)
            continue
        for t in range(4):
            zh = z[:, t * HEAD_DIM:(t + 1) * HEAD_DIM]
            r = zh * cos + pltpu.roll(zh, HEAD_DIM // 2, 1) * sin
            if c * chunk < Q_WIDTH:
                q_ref[:, c * chunk + t * HEAD_DIM:c * chunk + (t + 1) * HEAD_DIM] = (r * Q_PRESCALE).astype(BF16)
            else:
                k_ref[:, t * HEAD_DIM:(t + 1) * HEAD_DIM] = r.astype(BF16)


def _qkv_proj(x, mod, row0, seq, w_qkv, cos_t, sin_t, layer):
    m = x.shape[0]
    tm = TOKEN_TILE
    n = Q_WIDTH + 2 * KV_WIDTH
    return pl.pallas_call(
        _qkv_kernel,
        grid=(m // tm,),
        in_specs=[
            pl.BlockSpec((tm, D_MODEL), lambda i: (i, 0)),
            pl.BlockSpec((None, None, N_MOD, D_MODEL), lambda i: (layer, row0 + (i * tm) // seq, 0, 0)),
            pl.BlockSpec((None, D_MODEL, n), lambda i: (layer, 0, 0), pipeline_mode=pl.Buffered(1)),
            pl.BlockSpec((tm, HEAD_DIM), lambda i: (i % (seq // tm), 0)),
            pl.BlockSpec((tm, HEAD_DIM), lambda i: (i % (seq // tm), 0)),
        ],
        out_specs=[
            pl.BlockSpec((tm, Q_WIDTH), lambda i: (i, 0)),
            pl.BlockSpec((tm, KV_WIDTH), lambda i: (i, 0)),
            pl.BlockSpec((tm, KV_WIDTH), lambda i: (i, 0)),
        ],
        out_shape=[
            jax.ShapeDtypeStruct((m, Q_WIDTH), BF16),
            jax.ShapeDtypeStruct((m, KV_WIDTH), BF16),
            jax.ShapeDtypeStruct((m, KV_WIDTH), BF16),
        ],
        compiler_params=_params(("parallel",)),
        name=f"qkv_l{layer}_m{m}",
    )(x, mod, w_qkv, cos_t, sin_t)


def _sgu_kernel(x_ref, mod_ref, w_ref, lng_ref, lnb_ref, ws_ref, bs_ref, o_ref, vn_ref):
    tm = x_ref.shape[0]
    nchunk = tm // SGU_CHUNK
    gpc = 4

    shift, scale = mod_ref[3:4, :], mod_ref[4:5, :]
    h_parts = []
    for r in range(SGU_ROW_PARTS):
        rows = slice(r * (tm // SGU_ROW_PARTS), (r + 1) * (tm // SGU_ROW_PARTS))
        h_parts.append((x_ref[rows, :] * (1.0 + scale) + shift).astype(BF16))
        sv = _gelu_tanh(_dot(h_parts[-1], w_ref[:, SGU_WIDTH:2 * SGU_WIDTH]))
        vn_ref[rows, :] = _layer_norm(sv, lng_ref[...], lnb_ref[...]).astype(BF16)
    h = jnp.concatenate(h_parts, axis=0)

    def u_chunk(c):
        return _gelu_tanh(_dot(h, w_ref[:, c * gpc * LANE:(c + 1) * gpc * LANE]))

    pending = [u_chunk(c) for c in range(SGU_LOOKAHEAD)]
    for c in range(SGU_GROUPS // gpc):
        if c + SGU_LOOKAHEAD < SGU_GROUPS // gpc:
            pending.append(u_chunk(c + SGU_LOOKAHEAD))
        u = pending.pop(0)
        for t in range(gpc):
            g = c * gpc + t
            col = slice(g * LANE, (g + 1) * LANE)
            rhs = jnp.concatenate([vn_ref[n * SGU_CHUNK:(n + 1) * SGU_CHUNK, col] for n in range(nchunk)], axis=1)
            z = _dot(ws_ref[g], rhs)
            for n in range(nchunk):
                rows = slice(n * SGU_CHUNK, (n + 1) * SGU_CHUNK)
                zz = z[:, n * LANE:(n + 1) * LANE] + bs_ref[g]
                o_ref[rows, col] = (u[rows, t * LANE:(t + 1) * LANE] * zz).astype(BF16)


def _sgu_branch(x, mod, row0, seq, w_usv, sgu_ln_g, sgu_ln_b, sgu_w, sgu_b_bc, layer):
    m = x.shape[0]
    tm = TOKEN_TILE
    return pl.pallas_call(
        _sgu_kernel,
        grid=(m // tm,),
        in_specs=[
            pl.BlockSpec((tm, D_MODEL), lambda i: (i, 0)),
            pl.BlockSpec((None, None, N_MOD, D_MODEL), lambda i: (layer, row0 + (i * tm) // seq, 0, 0)),
            pl.BlockSpec((None, D_MODEL, 2 * SGU_WIDTH), lambda i: (layer, 0, 0), pipeline_mode=pl.Buffered(1)),
            pl.BlockSpec((None, 1, SGU_WIDTH), lambda i: (layer, 0, 0)),
            pl.BlockSpec((None, 1, SGU_WIDTH), lambda i: (layer, 0, 0)),
            pl.BlockSpec((None, SGU_GROUPS, SGU_CHUNK, SGU_CHUNK), lambda i: (layer, 0, 0, 0)),
            pl.BlockSpec((None, SGU_GROUPS, SGU_CHUNK, LANE), lambda i: (layer, 0, 0, 0)),
        ],
        out_specs=pl.BlockSpec((tm, SGU_WIDTH), lambda i: (i, 0)),
        out_shape=jax.ShapeDtypeStruct((m, SGU_WIDTH), BF16),
        scratch_shapes=[pltpu.VMEM((tm, SGU_WIDTH), BF16)],
        compiler_params=_params(("parallel",)),
        name=f"sgu_l{layer}_m{m}",
    )(x, mod, w_usv, sgu_ln_g, sgu_ln_b, sgu_w, sgu_b_bc)


def _attn_kernel(sink_ref, q_ref, kp_ref, kc_ref, kn_ref, vp_ref, vc_ref, vn_ref, o_ref, *, layer, seq):
    tq = q_ref.shape[0]
    pos0 = (pl.program_id(0) * tq) % seq
    kall = jnp.concatenate([kp_ref[...], kc_ref[...], kn_ref[...]], axis=0)
    vall = jnp.concatenate([vp_ref[...], vc_ref[...], vn_ref[...]], axis=0)
    span = 3 * ATTN_BLOCK
    kj = lax.broadcasted_iota(jnp.int32, (span, ATTN_BLOCK), 0)
    qi = lax.broadcasted_iota(jnp.int32, (span, ATTN_BLOCK), 1)
    band = jnp.abs(kj - ATTN_BLOCK - qi) <= WINDOW
    tail_row = lax.broadcasted_iota(jnp.int32, (ATTN_TAIL, 2 * HEAD_DIM), 0)
    tail_col = lax.broadcasted_iota(jnp.int32, (ATTN_TAIL, 2 * HEAD_DIM), 1)
    v_tail = jnp.where((tail_row < 2) & (tail_col >= HEAD_DIM), 1.0, 0.0).astype(BF16)
    ones_blk = jnp.ones((span, HEAD_DIM), BF16)
    p_row = lax.broadcasted_iota(jnp.int32, (ATTN_TAIL, Q_PER_KV * ATTN_BLOCK), 0)
    def band_bias(n):
        kpos = pos0 + (n - 1) * ATTN_BLOCK + kj
        ok = band & (kpos >= 0) & (kpos < seq)
        bias = jnp.where(ok, 0.0, -jnp.inf)
        bias = jnp.concatenate([bias] * Q_PER_KV, axis=1)
        return bias[:ATTN_BLOCK], bias[2 * ATTN_BLOCK:]

    def scores(n, g):
        rows = slice(n * ATTN_BLOCK, (n + 1) * ATTN_BLOCK)
        kg = kall[n * ATTN_BLOCK:n * ATTN_BLOCK + span, g * HEAD_DIM:(g + 1) * HEAD_DIM]
        heads = [g * Q_PER_KV + r for r in range(Q_PER_KV)]
        qs = jnp.concatenate([q_ref[rows, hh * HEAD_DIM:(hh + 1) * HEAD_DIM] for hh in heads], axis=0)
        return lax.dot_general(kg, qs, (((1,), (1,)), ((), ())), preferred_element_type=F32)

    def finish(n, g, s, bias_prev, bias_next):
        rows = slice(n * ATTN_BLOCK, (n + 1) * ATTN_BLOCK)
        vg = vall[n * ATTN_BLOCK:n * ATTN_BLOCK + span, g * HEAD_DIM:(g + 1) * HEAD_DIM]
        v_aug = jnp.concatenate([jnp.concatenate([vg, ones_blk], axis=1), v_tail], axis=0)
        heads = [g * Q_PER_KV + r for r in range(Q_PER_KV)]
        blocks = [s[:ATTN_BLOCK] + bias_prev, s[ATTN_BLOCK:2 * ATTN_BLOCK], s[2 * ATTN_BLOCK:] + bias_next]
        sink = jnp.concatenate(
            [jnp.full((1, ATTN_BLOCK), sink_ref[layer, hh] * LOG2E, F32) for hh in heads], axis=1)
        mx = sink
        for sb in blocks:
            mx = jnp.maximum(mx, jnp.max(sb, axis=0, keepdims=True))
        es = jnp.exp2(sink - mx)
        es_hi = es.astype(BF16).astype(F32)
        p_tail = jnp.where(p_row == 0, es_hi, jnp.where(p_row == 1, es - es_hi, 0.0)).astype(BF16)
        p_aug = jnp.concatenate([jnp.exp2(sb - mx).astype(BF16) for sb in blocks] + [p_tail], axis=0)
        o = lax.dot_general(p_aug, v_aug, (((0,), (0,)), ((), ())), preferred_element_type=F32)
        o = o[:, :HEAD_DIM] / o[:, HEAD_DIM:]
        for r, hh in enumerate(heads):
            o_ref[rows, hh * HEAD_DIM:(hh + 1) * HEAD_DIM] = o[r * ATTN_BLOCK:(r + 1) * ATTN_BLOCK].astype(BF16)

    units = [(n, g) for n in range(tq // ATTN_BLOCK) for g in range(N_KV_HEADS)]
    biases = {}
    pending = [scores(*u) for u in units[:ATTN_LOOKAHEAD]]
    for k, (n, g) in enumerate(units):
        if k + ATTN_LOOKAHEAD < len(units):
            pending.append(scores(*units[k + ATTN_LOOKAHEAD]))
        if n not in biases:
            biases[n] = band_bias(n)
        finish(n, g, pending.pop(0), *biases[n])


def _attention(q, k, v, attn_sink, seq, layer):
    m = q.shape[0]
    tq = TOKEN_TILE
    bpt = tq // ATTN_BLOCK
    last = m // ATTN_BLOCK - 1
    prev_map = lambda i: (jnp.maximum(i * bpt - 1, 0), 0)
    next_map = lambda i: (jnp.minimum(i * bpt + bpt, last), 0)
    cur_map = lambda i: (i, 0)
    return pl.pallas_call(
        functools.partial(_attn_kernel, layer=layer, seq=seq),
        grid=(m // tq,),
        in_specs=[
            pl.BlockSpec(memory_space=pltpu.SMEM),
            pl.BlockSpec((tq, Q_WIDTH), cur_map),
            pl.BlockSpec((ATTN_BLOCK, KV_WIDTH), prev_map),
            pl.BlockSpec((tq, KV_WIDTH), cur_map),
            pl.BlockSpec((ATTN_BLOCK, KV_WIDTH), next_map),
            pl.BlockSpec((ATTN_BLOCK, KV_WIDTH), prev_map),
            pl.BlockSpec((tq, KV_WIDTH), cur_map),
            pl.BlockSpec((ATTN_BLOCK, KV_WIDTH), next_map),
        ],
        out_specs=pl.BlockSpec((tq, Q_WIDTH), cur_map),
        out_shape=jax.ShapeDtypeStruct((m, Q_WIDTH), BF16),
        compiler_params=_params(("parallel",)),
        name=f"attn_l{layer}_m{m}",
    )(attn_sink, q, k, k, k, v, v, v)


def _merge_kernel(x_ref, mod_ref, ao_ref, so_ref, wgate_hbm, wba_hbm, wbs_hbm, wmo_hbm, o_ref,
                  h_ref, m_buf, wga_buf, wgb_buf, wba_buf, wbs_buf, wmo_buf, br_sem, mo_sem, *, layer, nj, tn):
    i, n_tiles = pl.program_id(0), pl.num_programs(0)
    t0 = i * nj
    gate = mod_ref[5:6, :]

    def br_copies(chunk, slot):
        col = pl.multiple_of(chunk * tn, tn)
        return (
            pltpu.make_async_copy(wgate_hbm.at[layer, :, pl.ds(col, tn)], wga_buf.at[slot], br_sem.at[slot, 0]),
            pltpu.make_async_copy(wgate_hbm.at[layer, :, pl.ds(D_MODEL + col, tn)], wgb_buf.at[slot],
                                  br_sem.at[slot, 1]),
            pltpu.make_async_copy(wba_hbm.at[layer, :, pl.ds(col, tn)], wba_buf.at[slot], br_sem.at[slot, 2]),
            pltpu.make_async_copy(wbs_hbm.at[layer, :, pl.ds(col, tn)], wbs_buf.at[slot], br_sem.at[slot, 3]),
        )

    def mo_copy(chunk, slot):
        row = pl.multiple_of(chunk * tn, tn)
        return pltpu.make_async_copy(wmo_hbm.at[layer, pl.ds(row, tn), :], wmo_buf.at[slot], mo_sem.at[slot])

    def merged_chunk(slot, rows=slice(None)):
        h = h_ref[rows, :]
        gate_a = _sigmoid(_dot(h, wga_buf[slot]))
        gate_b = _sigmoid(_dot(h, wgb_buf[slot]))
        attn = _dot(ao_ref[rows, :], wba_buf[slot])
        sgu = _dot(so_ref[rows, :], wbs_buf[slot])
        return (gate_a * attn + gate_b * sgu).astype(BF16)

    def boundary(j):
        return _stream_boundary(j, t0, nj, i + 1 < n_tiles, br_copies, mo_copy)

    @pl.when(i == 0)
    def _():
        _stream_prime(br_copies, mo_copy)

    slot0 = t0 % 2
    tm = x_ref.shape[0]
    shift, scale = mod_ref[3:4, :], mod_ref[4:5, :]
    for r in range(MERGE_ROW_PARTS):
        rows = slice(r * (tm // MERGE_ROW_PARTS), (r + 1) * (tm // MERGE_ROW_PARTS))
        x = x_ref[rows, :]
        h_ref[rows, :] = (x * (1.0 + scale) + shift).astype(BF16)
        o_ref[rows, :] = DEEPNORM_ALPHA * x
        m_buf[slot0, rows, :] = merged_chunk(slot0, rows)

    def trip(j, carry):
        slot = boundary(j)
        o_ref[...] += gate * _dot(m_buf[slot], wmo_buf[slot])
        m_buf[1 - slot] = merged_chunk(1 - slot)
        return carry

    lax.fori_loop(0, nj - 1, trip, 0)

    slot = boundary(nj - 1)
    o_ref[...] += gate * _dot(m_buf[slot], wmo_buf[slot])


def _merge_sublayer(x, mod, row0, seq, attn_o, sgu_o, w_gates, w_br_attn, w_br_sgu, w_mix_out, layer):
    m = x.shape[0]
    tm, tn = TOKEN_TILE, MERGE_TILE
    nj = D_MODEL // tn
    assert nj >= 2 and m % tm == 0 and seq % tm == 0
    return pl.pallas_call(
        functools.partial(_merge_kernel, layer=layer, nj=nj, tn=tn),
        grid=(m // tm,),
        in_specs=[
            pl.BlockSpec((tm, D_MODEL), lambda i: (i, 0)),
            pl.BlockSpec((None, None, N_MOD, D_MODEL), lambda i: (layer, row0 + (i * tm) // seq, 0, 0)),
            pl.BlockSpec((tm, Q_WIDTH), lambda i: (i, 0)),
            pl.BlockSpec((tm, SGU_WIDTH), lambda i: (i, 0)),
            pl.BlockSpec(memory_space=pl.ANY),
            pl.BlockSpec(memory_space=pl.ANY),
            pl.BlockSpec(memory_space=pl.ANY),
            pl.BlockSpec(memory_space=pl.ANY),
        ],
        out_specs=pl.BlockSpec((tm, D_MODEL), lambda i: (i, 0)),
        out_shape=jax.ShapeDtypeStruct((m, D_MODEL), F32),
        scratch_shapes=[
            pltpu.VMEM((tm, D_MODEL), BF16),
            pltpu.VMEM((2, tm, tn), BF16),
            pltpu.VMEM((2, D_MODEL, tn), BF16),
            pltpu.VMEM((2, D_MODEL, tn), BF16),
            pltpu.VMEM((2, Q_WIDTH, tn), BF16),
            pltpu.VMEM((2, SGU_WIDTH, tn), BF16),
            pltpu.VMEM((2, tn, D_MODEL), BF16),
            pltpu.SemaphoreType.DMA((2, 4)),
            pltpu.SemaphoreType.DMA((2,)),
        ],
        compiler_params=_params(("arbitrary",)),
        name=f"merge_l{layer}_m{m}",
    )(x, mod, attn_o, sgu_o, w_gates, w_br_attn, w_br_sgu, w_mix_out)


def _rope_tables(seq):
    half = HEAD_DIM // 2
    inv_freq = 1.0 / (ROPE_THETA ** (jnp.arange(half, dtype=F32) / half))
    ang = jnp.arange(seq, dtype=F32)[:, None] * inv_freq[None, :]
    cos = jnp.cos(ang)
    sin = jnp.sin(ang)
    return jnp.concatenate([cos, cos], axis=1), jnp.concatenate([-sin, sin], axis=1)


def kernel(x_prompt, x_sample, c_prompt, c_sample, w_ada, b_ada, ln_g, ln_b, ffn1_w_in, ffn1_w_out, w_mix_in, attn_sink, sgu_ln_g, sgu_ln_b, sgu_w, sgu_b, w_br_attn, w_br_sgu, w_mix_out, ffn2_w_in, ffn2_w_out):
    n_prompt = c_prompt.shape[0]
    c_all = jnp.concatenate([c_prompt, c_sample], axis=0)
    c_all = jnp.pad(c_all, ((0, -c_all.shape[0] % SUBLANE), (0, 0)))
    mod = _ada_mod(c_all, w_ada, b_ada)

    ffn1_in, ffn1_out = ffn1_w_in.astype(BF16), ffn1_w_out.astype(BF16)
    ffn2_in, ffn2_out = ffn2_w_in.astype(BF16), ffn2_w_out.astype(BF16)
    w_qkv = w_mix_in[:, :, :OFF_U].astype(BF16)
    w_usv = w_mix_in[:, :, OFF_U:OFF_GA].astype(BF16)
    w_gates = w_mix_in[:, :, OFF_GA:].astype(BF16)
    w_bra, w_brs, w_mo = w_br_attn.astype(BF16), w_br_sgu.astype(BF16), w_mix_out.astype(BF16)
    sgu_w16 = sgu_w.astype(BF16)
    sgu_b_bc = jnp.broadcast_to(sgu_b[..., None], sgu_b.shape + (LANE,))
    ln_g4 = ln_g.reshape(DEPTH, 3, 1, D_MODEL)
    ln_b4 = ln_b.reshape(DEPTH, 3, 1, D_MODEL)
    sgu_g3 = sgu_ln_g.reshape(DEPTH, 1, SGU_WIDTH)
    sgu_b3 = sgu_ln_b.reshape(DEPTH, 1, SGU_WIDTH)

    def run(x3, row0):
        bsz, seq, _ = x3.shape
        x = x3.reshape(bsz * seq, D_MODEL)
        cos_t, sin_t = _rope_tables(seq)
        for l in range(DEPTH):
            x = _ffn_sublayer(x, mod, row0, seq, ffn1_in, ffn1_out, ln_g4, ln_b4, l, 0,
                              ln_in=(l - 1, 2) if l > 0 else None, ln_out=True)
            q, k, v = _qkv_proj(x, mod, row0, seq, w_qkv, cos_t, sin_t, l)
            sgu_o = _sgu_branch(x, mod, row0, seq, w_usv, sgu_g3, sgu_b3, sgu_w16, sgu_b_bc, l)
            attn_o = _attention(q, k, v, attn_sink, seq, l)
            x = _merge_sublayer(x, mod, row0, seq, attn_o, sgu_o, w_gates, w_bra, w_brs, w_mo, l)
            x = _ffn_sublayer(x, mod, row0, seq, ffn2_in, ffn2_out, ln_g4, ln_b4, l, 2,
                              ln_in=(l, 1), ln_out=l == DEPTH - 1)
        return x.reshape(bsz, seq, D_MODEL)

    return run(x_prompt, 0), run(x_sample, n_prompt)
```

```python
import functools

import jax
import jax.numpy as jnp
from jax import lax
from jax.experimental import pallas as pl
from jax.experimental.pallas import tpu as pltpu

D_MODEL = 2048
DEPTH = 2
HEAD_DIM = 128
N_Q_HEADS = 16
N_KV_HEADS = 4
Q_PER_KV = N_Q_HEADS // N_KV_HEADS
WINDOW = 128
ATTN_BLOCK = 128
ROPE_THETA = 10000.0
SGU_WIDTH = D_MODEL
SGU_CHUNK = 128
SGU_GROUPS = 16
D_FF = 5632
Q_WIDTH = N_Q_HEADS * HEAD_DIM
KV_WIDTH = N_KV_HEADS * HEAD_DIM
OFF_V = Q_WIDTH + KV_WIDTH
OFF_U = OFF_V + KV_WIDTH
OFF_GA = OFF_U + 2 * SGU_WIDTH
N_MOD = 9
DEEPNORM_ALPHA = (2 * DEPTH) ** 0.25
MACARON_WEIGHT = 0.5
LN_EPS = 1e-5
ATTN_SCALE = HEAD_DIM ** -0.5
LOG2E = 1.4426950408889634
Q_PRESCALE = ATTN_SCALE * LOG2E
SGU_ROW_PARTS = 4
SGU_LOOKAHEAD = 2
ATTN_LOOKAHEAD = 1
ATTN_TAIL = 16

BF16 = jnp.bfloat16
F32 = jnp.float32

LANE = 128
SUBLANE = 8
TOKEN_TILE = 512
WIDE_TOKEN_TILE = 1024
FFN_TOKEN_TILE = 1024
FFN_ROW_PARTS = 4
FF_TILE = 512
MERGE_TILE = 512
MERGE_ROW_PARTS = 2
ADA_TILE = 1024
VMEM_LIMIT = 56 * 1024 * 1024


def _dot(a, b):
    return jnp.dot(a, b, preferred_element_type=F32)


def _sigmoid(x):
    return 1.0 / (1.0 + jnp.exp(-x))


def _gelu_tanh(x):
    return 0.5 * x * (1.0 + jnp.tanh(0.7978845608028654 * (x + 0.044715 * (x * x * x))))


def _layer_norm(r, g, b):
    mu = jnp.mean(r, axis=-1, keepdims=True)
    xc = r - mu
    var = jnp.mean(xc * xc, axis=-1, keepdims=True)
    return xc * lax.rsqrt(var + LN_EPS) * g + b


def _modulate(x_ref, mod_ref, sub):
    shift = mod_ref[3 * sub:3 * sub + 1, :]
    scale = mod_ref[3 * sub + 1:3 * sub + 2, :]
    return (x_ref[...] * (1.0 + scale) + shift).astype(BF16)


def _params(sem):
    return pltpu.CompilerParams(dimension_semantics=sem, vmem_limit_bytes=VMEM_LIMIT)


def _ada_kernel(c_ref, w_ref, b_ref, o_ref):
    c = c_ref[...]
    s = (c * _sigmoid(c)).astype(BF16)
    o_ref[...] = _dot(s, w_ref[...].astype(BF16)) + b_ref[...]


def _ada_mod(c_all, w_ada, b_ada):
    rows = c_all.shape[0]
    n_out = N_MOD * D_MODEL
    out = pl.pallas_call(
        _ada_kernel,
        grid=(DEPTH, n_out // ADA_TILE),
        in_specs=[
            pl.BlockSpec((rows, D_MODEL), lambda l, j: (0, 0)),
            pl.BlockSpec((None, D_MODEL, ADA_TILE), lambda l, j: (l, 0, j)),
            pl.BlockSpec((None, 1, ADA_TILE), lambda l, j: (l, 0, j)),
        ],
        out_specs=pl.BlockSpec((None, rows, ADA_TILE), lambda l, j: (l, 0, j)),
        out_shape=jax.ShapeDtypeStruct((DEPTH, rows, n_out), F32),
        compiler_params=_params(("arbitrary", "arbitrary")),
        name="ada_mod",
    )(c_all, w_ada, b_ada.reshape(DEPTH, 1, n_out))
    return out.reshape(DEPTH, rows, N_MOD, D_MODEL)


def _stream_prime(first_copies, second_copy):
    for c in first_copies(0, 0) + first_copies(1, 1):
        c.start()
    second_copy(0, 0).start()
    for c in first_copies(0, 0):
        c.wait()


def _stream_boundary(j, t0, n_chunks, has_next_tile, first_copies, second_copy):
    slot = (t0 + j) % 2
    second_copy(j, slot).wait()

    @pl.when((j + 1 < n_chunks) | has_next_tile)
    def _():
        for c in first_copies((j + 1) % n_chunks, 1 - slot):
            c.wait()
        second_copy((j + 1) % n_chunks, 1 - slot).start()

    @pl.when((j + 2 < n_chunks) | has_next_tile)
    def _():
        for c in first_copies((j + 2) % n_chunks, slot):
            c.start()
    return slot


def _ffn_kernel(x_hbm, mod_ref, win_hbm, wout_hbm, lng_in_ref, lnb_in_ref, lng_ref, lnb_ref, o_ref,
                x_buf, h_ref, a_buf, wg_buf, wu_buf, wo_buf, x_sem, gu_sem, wo_sem,
                *, layer, sub, nf, tf, ln_in, ln_out):
    i, n_tiles = pl.program_id(0), pl.num_programs(0)
    tm = x_buf.shape[0]
    t0 = i * nf
    res_gate = MACARON_WEIGHT * mod_ref[3 * sub + 2:3 * sub + 3, :]
    part = tm // FFN_ROW_PARTS
    row_parts = [slice(r * part, (r + 1) * part) for r in range(FFN_ROW_PARTS)]

    def x_copy(tile):
        return pltpu.make_async_copy(x_hbm.at[pl.ds(tile * tm, tm), :], x_buf, x_sem)

    def gu_copies(chunk, slot):
        col = pl.multiple_of(chunk * tf, tf)
        return (pltpu.make_async_copy(win_hbm.at[layer, :, pl.ds(col, tf)], wg_buf.at[slot], gu_sem.at[slot, 0]),
                pltpu.make_async_copy(win_hbm.at[layer, :, pl.ds(D_FF + col, tf)], wu_buf.at[slot],
                                      gu_sem.at[slot, 1]))

    def wo_copy(chunk, slot):
        row = pl.multiple_of(chunk * tf, tf)
        return pltpu.make_async_copy(wout_hbm.at[layer, pl.ds(row, tf), :], wo_buf.at[slot], wo_sem.at[slot])

    def gate_up(h, slot):
        g = _dot(h, wg_buf[slot])
        u = _dot(h, wu_buf[slot])
        return (g * _sigmoid(g) * u).astype(BF16)

    def boundary(j):
        return _stream_boundary(j, t0, nf, i + 1 < n_tiles, gu_copies, wo_copy)

    @pl.when(i == 0)
    def _():
        x_copy(0).start()
        _stream_prime(gu_copies, wo_copy)

    x_copy(i).wait()
    slot0 = t0 % 2
    shift = mod_ref[3 * sub:3 * sub + 1, :]
    scale = mod_ref[3 * sub + 1:3 * sub + 2, :]
    for rows in row_parts:
        x = x_buf[rows, :]
        if ln_in:
            x = _layer_norm(x, lng_in_ref[...], lnb_in_ref[...])
        h = (x * (1.0 + scale) + shift).astype(BF16)
        h_ref[rows, :] = h
        o_ref[rows, :] = DEEPNORM_ALPHA * x
        a_buf[slot0, rows, :] = gate_up(h, slot0)

    @pl.when(i + 1 < n_tiles)
    def _():
        x_copy(i + 1).start()

    def trip(j, carry):
        slot = boundary(j)
        o_ref[...] += res_gate * _dot(a_buf[slot], wo_buf[slot])
        a_buf[1 - slot] = gate_up(h_ref[...], 1 - slot)
        return carry

    lax.fori_loop(0, nf - 1, trip, 0)

    slot = boundary(nf - 1)
    res = [o_ref[rows, :] + res_gate * _dot(a_buf[slot, rows, :], wo_buf[slot]) for rows in row_parts]
    for rows, r in zip(row_parts, res):
        o_ref[rows, :] = _layer_norm(r, lng_ref[...], lnb_ref[...]) if ln_out else r


def _ffn_sublayer(x, mod, row0, seq, w_in, w_out, ln_g, ln_b, layer, sub, ln_in, ln_out):
    m = x.shape[0]
    tm, tf = FFN_TOKEN_TILE, FF_TILE
    nf = D_FF // tf
    assert nf >= 2 and m % tm == 0 and seq % tm == 0
    ln_src = ln_in if ln_in is not None else (layer, sub)
    return pl.pallas_call(
        functools.partial(_ffn_kernel, layer=layer, sub=sub, nf=nf, tf=tf, ln_in=ln_in is not None, ln_out=ln_out),
        grid=(m // tm,),
        in_specs=[
            pl.BlockSpec(memory_space=pl.ANY),
            pl.BlockSpec((None, None, N_MOD, D_MODEL), lambda i: (layer, row0 + (i * tm) // seq, 0, 0)),
            pl.BlockSpec(memory_space=pl.ANY),
            pl.BlockSpec(memory_space=pl.ANY),
            pl.BlockSpec((None, None, 1, D_MODEL), lambda i: (*ln_src, 0, 0)),
            pl.BlockSpec((None, None, 1, D_MODEL), lambda i: (*ln_src, 0, 0)),
            pl.BlockSpec((None, None, 1, D_MODEL), lambda i: (layer, sub, 0, 0)),
            pl.BlockSpec((None, None, 1, D_MODEL), lambda i: (layer, sub, 0, 0)),
        ],
        out_specs=pl.BlockSpec((tm, D_MODEL), lambda i: (i, 0)),
        out_shape=jax.ShapeDtypeStruct((m, D_MODEL), F32),
        scratch_shapes=[
            pltpu.VMEM((tm, D_MODEL), F32),
            pltpu.VMEM((tm, D_MODEL), BF16),
            pltpu.VMEM((2, tm, tf), BF16),
            pltpu.VMEM((2, D_MODEL, tf), BF16),
            pltpu.VMEM((2, D_MODEL, tf), BF16),
            pltpu.VMEM((2, tf, D_MODEL), BF16),
            pltpu.SemaphoreType.DMA(()),
            pltpu.SemaphoreType.DMA((2, 2)),
            pltpu.SemaphoreType.DMA((2,)),
        ],
        compiler_params=_params(("arbitrary",)),
        name=f"ffn_l{layer}_s{sub}_m{m}",
    )(x, mod, w_in, w_out, ln_g, ln_b, ln_g, ln_b)


def _qkv_kernel(x_ref, mod_ref, w_ref, cos_ref, sin_ref, q_ref, k_ref, v_ref):
    h = _modulate(x_ref, mod_ref, 1)
    cos = cos_ref[...]
    sin = sin_ref[...]
    chunk = 4 * HEAD_DIM
    for c in range((Q_WIDTH + 2 * KV_WIDTH) // chunk):
        z = _dot(h, w_ref[:, c * chunk:(c + 1) * chunk])
        if c * chunk >= OFF_V:
            v_ref[...] = z.astype(BF16)
            continue
        for t in range(4):
            zh = z[:, t * HEAD_DIM:(t + 1) * HEAD_DIM]
            r = zh * cos + pltpu.roll(zh, HEAD_DIM // 2, 1) * sin
            if c * chunk < Q_WIDTH:
                q_ref[:, c * chunk + t * HEAD_DIM:c * chunk + (t + 1) * HEAD_DIM] = (r * Q_PRESCALE).astype(BF16)
            else:
                k_ref[:, t * HEAD_DIM:(t + 1) * HEAD_DIM] = r.astype(BF16)


def _qkv_proj(x, mod, row0, seq, w_qkv, cos_t, sin_t, layer):
    m = x.shape[0]
    tm = WIDE_TOKEN_TILE
    n = Q_WIDTH + 2 * KV_WIDTH
    return pl.pallas_call(
        _qkv_kernel,
        grid=(m // tm,),
        in_specs=[
            pl.BlockSpec((tm, D_MODEL), lambda i: (i, 0)),
            pl.BlockSpec((None, None, N_MOD, D_MODEL), lambda i: (layer, row0 + (i * tm) // seq, 0, 0)),
            pl.BlockSpec((None, D_MODEL, n), lambda i: (layer, 0, 0), pipeline_mode=pl.Buffered(1)),
            pl.BlockSpec((tm, HEAD_DIM), lambda i: (i % (seq // tm), 0)),
            pl.BlockSpec((tm, HEAD_DIM), lambda i: (i % (seq // tm), 0)),
        ],
        out_specs=[
            pl.BlockSpec((tm, Q_WIDTH), lambda i: (i, 0)),
            pl.BlockSpec((tm, KV_WIDTH), lambda i: (i, 0)),
            pl.BlockSpec((tm, KV_WIDTH), lambda i: (i, 0)),
        ],
        out_shape=[
            jax.ShapeDtypeStruct((m, Q_WIDTH), BF16),
            jax.ShapeDtypeStruct((m, KV_WIDTH), BF16),
            jax.ShapeDtypeStruct((m, KV_WIDTH), BF16),
        ],
        compiler_params=_params(("parallel",)),
        name=f"qkv_l{layer}_m{m}",
    )(x, mod, w_qkv, cos_t, sin_t)


def _sgu_kernel(x_ref, mod_ref, w_ref, lng_ref, lnb_ref, ws_ref, bs_ref, o_ref, vn_ref):
    tm = x_ref.shape[0]
    nchunk = tm // SGU_CHUNK
    gpc = 4

    shift, scale = mod_ref[3:4, :], mod_ref[4:5, :]
    h_parts = []
    for r in range(SGU_ROW_PARTS):
        rows = slice(r * (tm // SGU_ROW_PARTS), (r + 1) * (tm // SGU_ROW_PARTS))
        h_parts.append((x_ref[rows, :] * (1.0 + scale) + shift).astype(BF16))
        sv = _gelu_tanh(_dot(h_parts[-1], w_ref[:, SGU_WIDTH:2 * SGU_WIDTH]))
        vn_ref[rows, :] = _layer_norm(sv, lng_ref[...], lnb_ref[...]).astype(BF16)
    h = jnp.concatenate(h_parts, axis=0)

    def u_chunk(c):
        return _gelu_tanh(_dot(h, w_ref[:, c * gpc * LANE:(c + 1) * gpc * LANE]))

    pending = [u_chunk(c) for c in range(SGU_LOOKAHEAD)]
    for c in range(SGU_GROUPS // gpc):
        if c + SGU_LOOKAHEAD < SGU_GROUPS // gpc:
            pending.append(u_chunk(c + SGU_LOOKAHEAD))
        u = pending.pop(0)
        for t in range(gpc):
            g = c * gpc + t
            col = slice(g * LANE, (g + 1) * LANE)
            rhs = jnp.concatenate([vn_ref[n * SGU_CHUNK:(n + 1) * SGU_CHUNK, col] for n in range(nchunk)], axis=1)
            z = _dot(ws_ref[g], rhs)
            for n in range(nchunk):
                rows = slice(n * SGU_CHUNK, (n + 1) * SGU_CHUNK)
                zz = z[:, n * LANE:(n + 1) * LANE] + bs_ref[g]
                o_ref[rows, col] = (u[rows, t * LANE:(t + 1) * LANE] * zz).astype(BF16)


def _sgu_branch(x, mod, row0, seq, w_usv, sgu_ln_g, sgu_ln_b, sgu_w, sgu_b_bc, layer):
    m = x.shape[0]
    tm = TOKEN_TILE
    return pl.pallas_call(
        _sgu_kernel,
        grid=(m // tm,),
        in_specs=[
            pl.BlockSpec((tm, D_MODEL), lambda i: (i, 0)),
            pl.BlockSpec((None, None, N_MOD, D_MODEL), lambda i: (layer, row0 + (i * tm) // seq, 0, 0)),
            pl.BlockSpec((None, D_MODEL, 2 * SGU_WIDTH), lambda i: (layer, 0, 0), pipeline_mode=pl.Buffered(1)),
            pl.BlockSpec((None, 1, SGU_WIDTH), lambda i: (layer, 0, 0)),
            pl.BlockSpec((None, 1, SGU_WIDTH), lambda i: (layer, 0, 0)),
            pl.BlockSpec((None, SGU_GROUPS, SGU_CHUNK, SGU_CHUNK), lambda i: (layer, 0, 0, 0)),
            pl.BlockSpec((None, SGU_GROUPS, SGU_CHUNK, LANE), lambda i: (layer, 0, 0, 0)),
        ],
        out_specs=pl.BlockSpec((tm, SGU_WIDTH), lambda i: (i, 0)),
        out_shape=jax.ShapeDtypeStruct((m, SGU_WIDTH), BF16),
        scratch_shapes=[pltpu.VMEM((tm, SGU_WIDTH), BF16)],
        compiler_params=_params(("parallel",)),
        name=f"sgu_l{layer}_m{m}",
    )(x, mod, w_usv, sgu_ln_g, sgu_ln_b, sgu_w, sgu_b_bc)


def _attn_kernel(sink_ref, q_ref, kp_ref, kc_ref, kn_ref, vp_ref, vc_ref, vn_ref, o_ref, *, layer, seq):
    tq = q_ref.shape[0]
    pos0 = (pl.program_id(0) * tq) % seq
    kall = jnp.concatenate([kp_ref[...], kc_ref[...], kn_ref[...]], axis=0)
    vall = jnp.concatenate([vp_ref[...], vc_ref[...], vn_ref[...]], axis=0)
    span = 3 * ATTN_BLOCK
    kj = lax.broadcasted_iota(jnp.int32, (span, ATTN_BLOCK), 0)
    qi = lax.broadcasted_iota(jnp.int32, (span, ATTN_BLOCK), 1)
    band = jnp.abs(kj - ATTN_BLOCK - qi) <= WINDOW
    tail_row = lax.broadcasted_iota(jnp.int32, (ATTN_TAIL, 2 * HEAD_DIM), 0)
    tail_col = lax.broadcasted_iota(jnp.int32, (ATTN_TAIL, 2 * HEAD_DIM), 1)
    v_tail = jnp.where((tail_row < 2) & (tail_col >= HEAD_DIM), 1.0, 0.0).astype(BF16)
    ones_blk = jnp.ones((span, HEAD_DIM), BF16)
    p_row = lax.broadcasted_iota(jnp.int32, (ATTN_TAIL, Q_PER_KV * ATTN_BLOCK), 0)
    def band_bias(n):
        kpos = pos0 + (n - 1) * ATTN_BLOCK + kj
        ok = band & (kpos >= 0) & (kpos < seq)
        bias = jnp.where(ok, 0.0, -jnp.inf)
        bias = jnp.concatenate([bias] * Q_PER_KV, axis=1)
        return bias[:ATTN_BLOCK], bias[2 * ATTN_BLOCK:]

    def scores(n, g):
        rows = slice(n * ATTN_BLOCK, (n + 1) * ATTN_BLOCK)
        kg = kall[n * ATTN_BLOCK:n * ATTN_BLOCK + span, g * HEAD_DIM:(g + 1) * HEAD_DIM]
        heads = [g * Q_PER_KV + r for r in range(Q_PER_KV)]
        qs = jnp.concatenate([q_ref[rows, hh * HEAD_DIM:(hh + 1) * HEAD_DIM] for hh in heads], axis=0)
        return lax.dot_general(kg, qs, (((1,), (1,)), ((), ())), preferred_element_type=F32)

    def finish(n, g, s, bias_prev, bias_next):
        rows = slice(n * ATTN_BLOCK, (n + 1) * ATTN_BLOCK)
        vg = vall[n * ATTN_BLOCK:n * ATTN_BLOCK + span, g * HEAD_DIM:(g + 1) * HEAD_DIM]
        v_aug = jnp.concatenate([jnp.concatenate([vg, ones_blk], axis=1), v_tail], axis=0)
        heads = [g * Q_PER_KV + r for r in range(Q_PER_KV)]
        blocks = [s[:ATTN_BLOCK] + bias_prev, s[ATTN_BLOCK:2 * ATTN_BLOCK], s[2 * ATTN_BLOCK:] + bias_next]
        sink = jnp.concatenate(
            [jnp.full((1, ATTN_BLOCK), sink_ref[layer, hh] * LOG2E, F32) for hh in heads], axis=1)
        mx = sink
        for sb in blocks:
            mx = jnp.maximum(mx, jnp.max(sb, axis=0, keepdims=True))
        es = jnp.exp2(sink - mx)
        es_hi = es.astype(BF16).astype(F32)
        p_tail = jnp.where(p_row == 0, es_hi, jnp.where(p_row == 1, es - es_hi, 0.0)).astype(BF16)
        p_aug = jnp.concatenate([jnp.exp2(sb - mx).astype(BF16) for sb in blocks] + [p_tail], axis=0)
        o = lax.dot_general(p_aug, v_aug, (((0,), (0,)), ((), ())), preferred_element_type=F32)
        o = o[:, :HEAD_DIM] / o[:, HEAD_DIM:]
        for r, hh in enumerate(heads):
            o_ref[rows, hh * HEAD_DIM:(hh + 1) * HEAD_DIM] = o[r * ATTN_BLOCK:(r + 1) * ATTN_BLOCK].astype(BF16)

    units = [(n, g) for n in range(tq // ATTN_BLOCK) for g in range(N_KV_HEADS)]
    biases = {}
    pending = [scores(*u) for u in units[:ATTN_LOOKAHEAD]]
    for k, (n, g) in enumerate(units):
        if k + ATTN_LOOKAHEAD < len(units):
            pending.append(scores(*units[k + ATTN_LOOKAHEAD]))
        if n not in biases:
            biases[n] = band_bias(n)
        finish(n, g, pending.pop(0), *biases[n])


def _attention(q, k, v, attn_sink, seq, layer):
    m = q.shape[0]
    tq = WIDE_TOKEN_TILE
    bpt = tq // ATTN_BLOCK
    last = m // ATTN_BLOCK - 1
    prev_map = lambda i: (jnp.maximum(i * bpt - 1, 0), 0)
    next_map = lambda i: (jnp.minimum(i * bpt + bpt, last), 0)
    cur_map = lambda i: (i, 0)
    return pl.pallas_call(
        functools.partial(_attn_kernel, layer=layer, seq=seq),
        grid=(m // tq,),
        in_specs=[
            pl.BlockSpec(memory_space=pltpu.SMEM),
            pl.BlockSpec((tq, Q_WIDTH), cur_map),
            pl.BlockSpec((ATTN_BLOCK, KV_WIDTH), prev_map),
            pl.BlockSpec((tq, KV_WIDTH), cur_map),
            pl.BlockSpec((ATTN_BLOCK, KV_WIDTH), next_map),
            pl.BlockSpec((ATTN_BLOCK, KV_WIDTH), prev_map),
            pl.BlockSpec((tq, KV_WIDTH), cur_map),
            pl.BlockSpec((ATTN_BLOCK, KV_WIDTH), next_map),
        ],
        out_specs=pl.BlockSpec((tq, Q_WIDTH), cur_map),
        out_shape=jax.ShapeDtypeStruct((m, Q_WIDTH), BF16),
        compiler_params=_params(("parallel",)),
        name=f"attn_l{layer}_m{m}",
    )(attn_sink, q, k, k, k, v, v, v)


def _merge_kernel(x_ref, mod_ref, ao_ref, so_ref, wgate_hbm, wba_hbm, wbs_hbm, wmo_hbm, o_ref,
                  h_ref, m_buf, wga_buf, wgb_buf, wba_buf, wbs_buf, wmo_buf, br_sem, mo_sem, *, layer, nj, tn):
    i, n_tiles = pl.program_id(0), pl.num_programs(0)
    t0 = i * nj
    gate = mod_ref[5:6, :]

    def br_copies(chunk, slot):
        col = pl.multiple_of(chunk * tn, tn)
        return (
            pltpu.make_async_copy(wgate_hbm.at[layer, :, pl.ds(col, tn)], wga_buf.at[slot], br_sem.at[slot, 0]),
            pltpu.make_async_copy(wgate_hbm.at[layer, :, pl.ds(D_MODEL + col, tn)], wgb_buf.at[slot],
                                  br_sem.at[slot, 1]),
            pltpu.make_async_copy(wba_hbm.at[layer, :, pl.ds(col, tn)], wba_buf.at[slot], br_sem.at[slot, 2]),
            pltpu.make_async_copy(wbs_hbm.at[layer, :, pl.ds(col, tn)], wbs_buf.at[slot], br_sem.at[slot, 3]),
        )

    def mo_copy(chunk, slot):
        row = pl.multiple_of(chunk * tn, tn)
        return pltpu.make_async_copy(wmo_hbm.at[layer, pl.ds(row, tn), :], wmo_buf.at[slot], mo_sem.at[slot])

    def merged_chunk(slot, rows=slice(None)):
        h = h_ref[rows, :]
        gate_a = _sigmoid(_dot(h, wga_buf[slot]))
        gate_b = _sigmoid(_dot(h, wgb_buf[slot]))
        attn = _dot(ao_ref[rows, :], wba_buf[slot])
        sgu = _dot(so_ref[rows, :], wbs_buf[slot])
        return (gate_a * attn + gate_b * sgu).astype(BF16)

    def boundary(j):
        return _stream_boundary(j, t0, nj, i + 1 < n_tiles, br_copies, mo_copy)

    @pl.when(i == 0)
    def _():
        _stream_prime(br_copies, mo_copy)

    slot0 = t0 % 2
    tm = x_ref.shape[0]
    shift, scale = mod_ref[3:4, :], mod_ref[4:5, :]
    for r in range(MERGE_ROW_PARTS):
        rows = slice(r * (tm // MERGE_ROW_PARTS), (r + 1) * (tm // MERGE_ROW_PARTS))
        x = x_ref[rows, :]
        h_ref[rows, :] = (x * (1.0 + scale) + shift).astype(BF16)
        o_ref[rows, :] = DEEPNORM_ALPHA * x
        m_buf[slot0, rows, :] = merged_chunk(slot0, rows)

    def trip(j, carry):
        slot = boundary(j)
        o_ref[...] += gate * _dot(m_buf[slot], wmo_buf[slot])
        m_buf[1 - slot] = merged_chunk(1 - slot)
        return carry

    lax.fori_loop(0, nj - 1, trip, 0)

    slot = boundary(nj - 1)
    o_ref[...] += gate * _dot(m_buf[slot], wmo_buf[slot])


def _merge_sublayer(x, mod, row0, seq, attn_o, sgu_o, w_gates, w_br_attn, w_br_sgu, w_mix_out, layer):
    m = x.shape[0]
    tm, tn = TOKEN_TILE, MERGE_TILE
    nj = D_MODEL // tn
    assert nj >= 2 and m % tm == 0 and seq % tm == 0
    return pl.pallas_call(
        functools.partial(_merge_kernel, layer=layer, nj=nj, tn=tn),
        grid=(m // tm,),
        in_specs=[
            pl.BlockSpec((tm, D_MODEL), lambda i: (i, 0)),
            pl.BlockSpec((None, None, N_MOD, D_MODEL), lambda i: (layer, row0 + (i * tm) // seq, 0, 0)),
            pl.BlockSpec((tm, Q_WIDTH), lambda i: (i, 0)),
            pl.BlockSpec((tm, SGU_WIDTH), lambda i: (i, 0)),
            pl.BlockSpec(memory_space=pl.ANY),
            pl.BlockSpec(memory_space=pl.ANY),
            pl.BlockSpec(memory_space=pl.ANY),
            pl.BlockSpec(memory_space=pl.ANY),
        ],
        out_specs=pl.BlockSpec((tm, D_MODEL), lambda i: (i, 0)),
        out_shape=jax.ShapeDtypeStruct((m, D_MODEL), F32),
        scratch_shapes=[
            pltpu.VMEM((tm, D_MODEL), BF16),
            pltpu.VMEM((2, tm, tn), BF16),
            pltpu.VMEM((2, D_MODEL, tn), BF16),
            pltpu.VMEM((2, D_MODEL, tn), BF16),
            pltpu.VMEM((2, Q_WIDTH, tn), BF16),
            pltpu.VMEM((2, SGU_WIDTH, tn), BF16),
            pltpu.VMEM((2, tn, D_MODEL), BF16),
            pltpu.SemaphoreType.DMA((2, 4)),
            pltpu.SemaphoreType.DMA((2,)),
        ],
        compiler_params=_params(("arbitrary",)),
        name=f"merge_l{layer}_m{m}",
    )(x, mod, attn_o, sgu_o, w_gates, w_br_attn, w_br_sgu, w_mix_out)


def _rope_tables(seq):
    half = HEAD_DIM // 2
    inv_freq = 1.0 / (ROPE_THETA ** (jnp.arange(half, dtype=F32) / half))
    ang = jnp.arange(seq, dtype=F32)[:, None] * inv_freq[None, :]
    cos = jnp.cos(ang)
    sin = jnp.sin(ang)
    return jnp.concatenate([cos, cos], axis=1), jnp.concatenate([-sin, sin], axis=1)


def kernel(x_prompt, x_sample, c_prompt, c_sample, w_ada, b_ada, ln_g, ln_b, ffn1_w_in, ffn1_w_out, w_mix_in, attn_sink, sgu_ln_g, sgu_ln_b, sgu_w, sgu_b, w_br_attn, w_br_sgu, w_mix_out, ffn2_w_in, ffn2_w_out):
    n_prompt = c_prompt.shape[0]
    c_all = jnp.concatenate([c_prompt, c_sample], axis=0)
    c_all = jnp.pad(c_all, ((0, -c_all.shape[0] % SUBLANE), (0, 0)))
    mod = _ada_mod(c_all, w_ada, b_ada)

    ffn1_in, ffn1_out = ffn1_w_in.astype(BF16), ffn1_w_out.astype(BF16)
    ffn2_in, ffn2_out = ffn2_w_in.astype(BF16), ffn2_w_out.astype(BF16)
    w_qkv = w_mix_in[:, :, :OFF_U].astype(BF16)
    w_usv = w_mix_in[:, :, OFF_U:OFF_GA].astype(BF16)
    w_gates = w_mix_in[:, :, OFF_GA:].astype(BF16)
    w_bra, w_brs, w_mo = w_br_attn.astype(BF16), w_br_sgu.astype(BF16), w_mix_out.astype(BF16)
    sgu_w16 = sgu_w.astype(BF16)
    sgu_b_bc = jnp.broadcast_to(sgu_b[..., None], sgu_b.shape + (LANE,))
    ln_g4 = ln_g.reshape(DEPTH, 3, 1, D_MODEL)
    ln_b4 = ln_b.reshape(DEPTH, 3, 1, D_MODEL)
    sgu_g3 = sgu_ln_g.reshape(DEPTH, 1, SGU_WIDTH)
    sgu_b3 = sgu_ln_b.reshape(DEPTH, 1, SGU_WIDTH)

    def run(x3, row0):
        bsz, seq, _ = x3.shape
        x = x3.reshape(bsz * seq, D_MODEL)
        cos_t, sin_t = _rope_tables(seq)
        for l in range(DEPTH):
            x = _ffn_sublayer(x, mod, row0, seq, ffn1_in, ffn1_out, ln_g4, ln_b4, l, 0,
                              ln_in=(l - 1, 2) if l > 0 else None, ln_out=True)
            q, k, v = _qkv_proj(x, mod, row0, seq, w_qkv, cos_t, sin_t, l)
            sgu_o = _sgu_branch(x, mod, row0, seq, w_usv, sgu_g3, sgu_b3, sgu_w16, sgu_b_bc, l)
            attn_o = _attention(q, k, v, attn_sink, seq, l)
            x = _merge_sublayer(x, mod, row0, seq, attn_o, sgu_o, w_gates, w_bra, w_brs, w_mo, l)
            x = _ffn_sublayer(x, mod, row0, seq, ffn2_in, ffn2_out, ln_g4, ln_b4, l, 2,
                              ln_in=(l, 1), ln_out=l == DEPTH - 1)
        return x.reshape(bsz, seq, D_MODEL)

    return run(x_prompt, 0), run(x_sample, n_prompt)
```

```python
import functools

import jax
import jax.numpy as jnp
from jax import lax
from jax.experimental import pallas as pl
from jax.experimental.pallas import tpu as pltpu

D_MODEL = 2048
DEPTH = 2
HEAD_DIM = 128
N_Q_HEADS = 16
N_KV_HEADS = 4
Q_PER_KV = N_Q_HEADS // N_KV_HEADS
WINDOW = 128
ATTN_BLOCK = 128
ROPE_THETA = 10000.0
SGU_WIDTH = D_MODEL
SGU_CHUNK = 128
SGU_GROUPS = 16
D_FF = 5632
Q_WIDTH = N_Q_HEADS * HEAD_DIM
KV_WIDTH = N_KV_HEADS * HEAD_DIM
OFF_V = Q_WIDTH + KV_WIDTH
OFF_U = OFF_V + KV_WIDTH
OFF_GA = OFF_U + 2 * SGU_WIDTH
N_MOD = 9
DEEPNORM_ALPHA = (2 * DEPTH) ** 0.25
MACARON_WEIGHT = 0.5
LN_EPS = 1e-5
ATTN_SCALE = HEAD_DIM ** -0.5
LOG2E = 1.4426950408889634
Q_PRESCALE = ATTN_SCALE * LOG2E
SGU_ROW_PARTS = 4
SGU_LOOKAHEAD = 2
ATTN_LOOKAHEAD = 1
ATTN_TAIL = 16

BF16 = jnp.bfloat16
F32 = jnp.float32

LANE = 128
SUBLANE = 8
TOKEN_TILE = 512
WIDE_TOKEN_TILE = 1024
FFN_TOKEN_TILE = 1024
FFN_ROW_PARTS = 4
FF_TILE = 512
MERGE_TILE = 512
MERGE_ROW_PARTS = 2
ADA_TILE = 1024
VMEM_LIMIT = 56 * 1024 * 1024


def _dot(a, b):
    return jnp.dot(a, b, preferred_element_type=F32)


def _sigmoid(x):
    return 1.0 / (1.0 + jnp.exp(-x))


def _gelu_tanh(x):
    return 0.5 * x * (1.0 + jnp.tanh(0.7978845608028654 * (x + 0.044715 * (x * x * x))))


def _layer_norm(r, g, b):
    mu = jnp.mean(r, axis=-1, keepdims=True)
    xc = r - mu
    var = jnp.mean(xc * xc, axis=-1, keepdims=True)
    return xc * lax.rsqrt(var + LN_EPS) * g + b


def _modulate(x_ref, mod_ref, sub):
    shift = mod_ref[3 * sub:3 * sub + 1, :]
    scale = mod_ref[3 * sub + 1:3 * sub + 2, :]
    return (x_ref[...] * (1.0 + scale) + shift).astype(BF16)


def _params(sem):
    return pltpu.CompilerParams(dimension_semantics=sem, vmem_limit_bytes=VMEM_LIMIT)


def _ada_kernel(c_ref, w_ref, b_ref, o_ref):
    c = c_ref[...]
    s = (c * _sigmoid(c)).astype(BF16)
    o_ref[...] = _dot(s, w_ref[...].astype(BF16)) + b_ref[...]


def _ada_mod(c_all, w_ada, b_ada):
    rows = c_all.shape[0]
    n_out = N_MOD * D_MODEL
    out = pl.pallas_call(
        _ada_kernel,
        grid=(DEPTH, n_out // ADA_TILE),
        in_specs=[
            pl.BlockSpec((rows, D_MODEL), lambda l, j: (0, 0)),
            pl.BlockSpec((None, D_MODEL, ADA_TILE), lambda l, j: (l, 0, j)),
            pl.BlockSpec((None, 1, ADA_TILE), lambda l, j: (l, 0, j)),
        ],
        out_specs=pl.BlockSpec((None, rows, ADA_TILE), lambda l, j: (l, 0, j)),
        out_shape=jax.ShapeDtypeStruct((DEPTH, rows, n_out), F32),
        compiler_params=_params(("arbitrary", "arbitrary")),
        name="ada_mod",
    )(c_all, w_ada, b_ada.reshape(DEPTH, 1, n_out))
    return out.reshape(DEPTH, rows, N_MOD, D_MODEL)


def _stream_prime(first_copies, second_copy):
    for c in first_copies(0, 0) + first_copies(1, 1):
        c.start()
    second_copy(0, 0).start()
    for c in first_copies(0, 0):
        c.wait()


def _stream_boundary(j, t0, n_chunks, has_next_tile, first_copies, second_copy):
    slot = (t0 + j) % 2
    second_copy(j, slot).wait()

    @pl.when((j + 1 < n_chunks) | has_next_tile)
    def _():
        for c in first_copies((j + 1) % n_chunks, 1 - slot):
            c.wait()
        second_copy((j + 1) % n_chunks, 1 - slot).start()

    @pl.when((j + 2 < n_chunks) | has_next_tile)
    def _():
        for c in first_copies((j + 2) % n_chunks, slot):
            c.start()
    return slot


def _ffn_kernel(x_hbm, mod_ref, win_hbm, wout_hbm, lng_in_ref, lnb_in_ref, lng_ref, lnb_ref, o_ref,
                x_buf, h_ref, a_buf, wg_buf, wu_buf, wo_buf, x_sem, gu_sem, wo_sem,
                *, layer, sub, nf, tf, ln_in, ln_out):
    i, n_tiles = pl.program_id(0), pl.num_programs(0)
    tm = x_buf.shape[0]
    res_gate = MACARON_WEIGHT * mod_ref[3 * sub + 2:3 * sub + 3, :]
    part = tm // FFN_ROW_PARTS
    row_parts = [slice(r * part, (r + 1) * part) for r in range(FFN_ROW_PARTS)]

    def x_copy(tile):
        return pltpu.make_async_copy(x_hbm.at[pl.ds(tile * tm, tm), :], x_buf, x_sem)

    def gu_copies(chunk, slot):
        col = pl.multiple_of(chunk * tf, tf)
        return (pltpu.make_async_copy(win_hbm.at[layer, :, pl.ds(col, tf)], wg_buf.at[slot], gu_sem.at[slot, 0]),
                pltpu.make_async_copy(win_hbm.at[layer, :, pl.ds(D_FF + col, tf)], wu_buf.at[slot],
                                      gu_sem.at[slot, 1]))

    def wo_copy(chunk, slot):
        row = pl.multiple_of(chunk * tf, tf)
        return pltpu.make_async_copy(wout_hbm.at[layer, pl.ds(row, tf), :], wo_buf.at[slot], wo_sem.at[slot])

    def gate_up(h, slot):
        g = _dot(h, wg_buf[slot])
        u = _dot(h, wu_buf[slot])
        return (g * _sigmoid(g) * u).astype(BF16)

    def boundary(j, slot):
        wo_copy(j, slot).wait()
        for c in gu_copies(j + 1, 1 - slot):
            c.wait()
        wo_copy(j + 1, 1 - slot).start()

        @pl.when(j + 2 < nf)
        def _():
            for c in gu_copies(j + 2, slot):
                c.start()

    def start_first_weights():
        for c in gu_copies(0, 0) + gu_copies(1, 1):
            c.start()

    @pl.when(i == 0)
    def _():
        x_copy(0).start()
        start_first_weights()
        wo_copy(0, 0).start()

    x_copy(i).wait()
    for c in gu_copies(0, 0):
        c.wait()
    slot0 = 0
    shift = mod_ref[3 * sub:3 * sub + 1, :]
    scale = mod_ref[3 * sub + 1:3 * sub + 2, :]
    for rows in row_parts:
        x = x_buf[rows, :]
        if ln_in:
            x = _layer_norm(x, lng_in_ref[...], lnb_in_ref[...])
        h = (x * (1.0 + scale) + shift).astype(BF16)
        h_ref[rows, :] = h
        o_ref[rows, :] = DEEPNORM_ALPHA * x
        a_buf[slot0, rows, :] = gate_up(h, slot0)

    @pl.when(i + 1 < n_tiles)
    def _():
        x_copy(i + 1).start()

    def trip(k, carry):
        for slot in (0, 1):
            boundary(2 * k + slot, slot)
            o_ref[...] += res_gate * _dot(a_buf[slot], wo_buf[slot])
            a_buf[1 - slot] = gate_up(h_ref[...], 1 - slot)
        return carry

    lax.fori_loop(0, (nf - 1) // 2, trip, 0)

    wo_copy(nf - 1, 0).wait()

    @pl.when(i + 1 < n_tiles)
    def _():
        start_first_weights()

    res = [o_ref[rows, :] + res_gate * _dot(a_buf[0, rows, :], wo_buf[0]) for rows in row_parts]
    for rows, r in zip(row_parts, res):
        o_ref[rows, :] = _layer_norm(r, lng_ref[...], lnb_ref[...]) if ln_out else r

    @pl.when(i + 1 < n_tiles)
    def _():
        wo_copy(0, 0).start()


def _ffn_sublayer(x, mod, row0, seq, w_in, w_out, ln_g, ln_b, layer, sub, ln_in, ln_out):
    m = x.shape[0]
    tm, tf = FFN_TOKEN_TILE, FF_TILE
    nf = D_FF // tf
    assert nf >= 3 and nf % 2 == 1 and m % tm == 0 and seq % tm == 0
    ln_src = ln_in if ln_in is not None else (layer, sub)
    return pl.pallas_call(
        functools.partial(_ffn_kernel, layer=layer, sub=sub, nf=nf, tf=tf, ln_in=ln_in is not None, ln_out=ln_out),
        grid=(m // tm,),
        in_specs=[
            pl.BlockSpec(memory_space=pl.ANY),
            pl.BlockSpec((None, None, N_MOD, D_MODEL), lambda i: (layer, row0 + (i * tm) // seq, 0, 0)),
            pl.BlockSpec(memory_space=pl.ANY),
            pl.BlockSpec(memory_space=pl.ANY),
            pl.BlockSpec((None, None, 1, D_MODEL), lambda i: (*ln_src, 0, 0)),
            pl.BlockSpec((None, None, 1, D_MODEL), lambda i: (*ln_src, 0, 0)),
            pl.BlockSpec((None, None, 1, D_MODEL), lambda i: (layer, sub, 0, 0)),
            pl.BlockSpec((None, None, 1, D_MODEL), lambda i: (layer, sub, 0, 0)),
        ],
        out_specs=pl.BlockSpec((tm, D_MODEL), lambda i: (i, 0)),
        out_shape=jax.ShapeDtypeStruct((m, D_MODEL), F32),
        scratch_shapes=[
            pltpu.VMEM((tm, D_MODEL), F32),
            pltpu.VMEM((tm, D_MODEL), BF16),
            pltpu.VMEM((2, tm, tf), BF16),
            pltpu.VMEM((2, D_MODEL, tf), BF16),
            pltpu.VMEM((2, D_MODEL, tf), BF16),
            pltpu.VMEM((2, tf, D_MODEL), BF16),
            pltpu.SemaphoreType.DMA(()),
            pltpu.SemaphoreType.DMA((2, 2)),
            pltpu.SemaphoreType.DMA((2,)),
        ],
        compiler_params=_params(("arbitrary",)),
        name=f"ffn_l{layer}_s{sub}_m{m}",
    )(x, mod, w_in, w_out, ln_g, ln_b, ln_g, ln_b)


def _qkv_kernel(x_ref, mod_ref, w_ref, cos_ref, sin_ref, q_ref, k_ref, v_ref):
    h = _modulate(x_ref, mod_ref, 1)
    cos = cos_ref[...]
    sin = sin_ref[...]
    chunk = 4 * HEAD_DIM
    for c in range((Q_WIDTH + 2 * KV_WIDTH) // chunk):
        z = _dot(h, w_ref[:, c * chunk:(c + 1) * chunk])
        if c * chunk >= OFF_V:
            v_ref[...] = z.astype(BF16)
            continue
        for t in range(4):
            zh = z[:, t * HEAD_DIM:(t + 1) * HEAD_DIM]
            r = zh * cos + pltpu.roll(zh, HEAD_DIM // 2, 1) * sin
            if c * chunk < Q_WIDTH:
                q_ref[:, c * chunk + t * HEAD_DIM:c * chunk + (t + 1) * HEAD_DIM] = (r * Q_PRESCALE).astype(BF16)
            else:
                k_ref[:, t * HEAD_DIM:(t + 1) * HEAD_DIM] = r.astype(BF16)


def _qkv_proj(x, mod, row0, seq, w_qkv, cos_t, sin_t, layer):
    m = x.shape[0]
    tm = WIDE_TOKEN_TILE
    n = Q_WIDTH + 2 * KV_WIDTH
    return pl.pallas_call(
        _qkv_kernel,
        grid=(m // tm,),
        in_specs=[
            pl.BlockSpec((tm, D_MODEL), lambda i: (i, 0)),
            pl.BlockSpec((None, None, N_MOD, D_MODEL), lambda i: (layer, row0 + (i * tm) // seq, 0, 0)),
            pl.BlockSpec((None, D_MODEL, n), lambda i: (layer, 0, 0), pipeline_mode=pl.Buffered(1)),
            pl.BlockSpec((tm, HEAD_DIM), lambda i: (i % (seq // tm), 0)),
            pl.BlockSpec((tm, HEAD_DIM), lambda i: (i % (seq // tm), 0)),
        ],
        out_specs=[
            pl.BlockSpec((tm, Q_WIDTH), lambda i: (i, 0)),
            pl.BlockSpec((tm, KV_WIDTH), lambda i: (i, 0)),
            pl.BlockSpec((tm, KV_WIDTH), lambda i: (i, 0)),
        ],
        out_shape=[
            jax.ShapeDtypeStruct((m, Q_WIDTH), BF16),
            jax.ShapeDtypeStruct((m, KV_WIDTH), BF16),
            jax.ShapeDtypeStruct((m, KV_WIDTH), BF16),
        ],
        compiler_params=_params(("parallel",)),
        name=f"qkv_l{layer}_m{m}",
    )(x, mod, w_qkv, cos_t, sin_t)


def _sgu_kernel(x_ref, mod_ref, w_ref, lng_ref, lnb_ref, ws_ref, bs_ref, o_ref, vn_ref):
    tm = x_ref.shape[0]
    nchunk = tm // SGU_CHUNK
    gpc = 4

    shift, scale = mod_ref[3:4, :], mod_ref[4:5, :]
    h_parts = []
    for r in range(SGU_ROW_PARTS):
        rows = slice(r * (tm // SGU_ROW_PARTS), (r + 1) * (tm // SGU_ROW_PARTS))
        h_parts.append((x_ref[rows, :] * (1.0 + scale) + shift).astype(BF16))
        sv = _gelu_tanh(_dot(h_parts[-1], w_ref[:, SGU_WIDTH:2 * SGU_WIDTH]))
        vn_ref[rows, :] = _layer_norm(sv, lng_ref[...], lnb_ref[...]).astype(BF16)
    h = jnp.concatenate(h_parts, axis=0)

    def u_chunk(c):
        return _gelu_tanh(_dot(h, w_ref[:, c * gpc * LANE:(c + 1) * gpc * LANE]))

    pending = [u_chunk(c) for c in range(SGU_LOOKAHEAD)]
    for c in range(SGU_GROUPS // gpc):
        if c + SGU_LOOKAHEAD < SGU_GROUPS // gpc:
            pending.append(u_chunk(c + SGU_LOOKAHEAD))
        u = pending.pop(0)
        for t in range(gpc):
            g = c * gpc + t
            col = slice(g * LANE, (g + 1) * LANE)
            rhs = jnp.concatenate([vn_ref[n * SGU_CHUNK:(n + 1) * SGU_CHUNK, col] for n in range(nchunk)], axis=1)
            z = _dot(ws_ref[g], rhs)
            for n in range(nchunk):
                rows = slice(n * SGU_CHUNK, (n + 1) * SGU_CHUNK)
                zz = z[:, n * LANE:(n + 1) * LANE] + bs_ref[g]
                o_ref[rows, col] = (u[rows, t * LANE:(t + 1) * LANE] * zz).astype(BF16)


def _sgu_branch(x, mod, row0, seq, w_usv, sgu_ln_g, sgu_ln_b, sgu_w, sgu_b_bc, layer):
    m = x.shape[0]
    tm = TOKEN_TILE
    return pl.pallas_call(
        _sgu_kernel,
        grid=(m // tm,),
        in_specs=[
            pl.BlockSpec((tm, D_MODEL), lambda i: (i, 0)),
            pl.BlockSpec((None, None, N_MOD, D_MODEL), lambda i: (layer, row0 + (i * tm) // seq, 0, 0)),
            pl.BlockSpec((None, D_MODEL, 2 * SGU_WIDTH), lambda i: (layer, 0, 0), pipeline_mode=pl.Buffered(1)),
            pl.BlockSpec((None, 1, SGU_WIDTH), lambda i: (layer, 0, 0)),
            pl.BlockSpec((None, 1, SGU_WIDTH), lambda i: (layer, 0, 0)),
            pl.BlockSpec((None, SGU_GROUPS, SGU_CHUNK, SGU_CHUNK), lambda i: (layer, 0, 0, 0)),
            pl.BlockSpec((None, SGU_GROUPS, SGU_CHUNK, LANE), lambda i: (layer, 0, 0, 0)),
        ],
        out_specs=pl.BlockSpec((tm, SGU_WIDTH), lambda i: (i, 0)),
        out_shape=jax.ShapeDtypeStruct((m, SGU_WIDTH), BF16),
        scratch_shapes=[pltpu.VMEM((tm, SGU_WIDTH), BF16)],
        compiler_params=_params(("parallel",)),
        name=f"sgu_l{layer}_m{m}",
    )(x, mod, w_usv, sgu_ln_g, sgu_ln_b, sgu_w, sgu_b_bc)


def _attn_kernel(sink_ref, q_ref, kp_ref, kc_ref, kn_ref, vp_ref, vc_ref, vn_ref, o_ref, *, layer, seq):
    tq = q_ref.shape[0]
    pos0 = (pl.program_id(0) * tq) % seq
    kall = jnp.concatenate([kp_ref[...], kc_ref[...], kn_ref[...]], axis=0)
    vall = jnp.concatenate([vp_ref[...], vc_ref[...], vn_ref[...]], axis=0)
    span = 3 * ATTN_BLOCK
    kj = lax.broadcasted_iota(jnp.int32, (span, ATTN_BLOCK), 0)
    qi = lax.broadcasted_iota(jnp.int32, (span, ATTN_BLOCK), 1)
    band = jnp.abs(kj - ATTN_BLOCK - qi) <= WINDOW
    tail_row = lax.broadcasted_iota(jnp.int32, (ATTN_TAIL, 2 * HEAD_DIM), 0)
    tail_col = lax.broadcasted_iota(jnp.int32, (ATTN_TAIL, 2 * HEAD_DIM), 1)
    v_tail = jnp.where((tail_row < 2) & (tail_col >= HEAD_DIM), 1.0, 0.0).astype(BF16)
    ones_blk = jnp.ones((span, HEAD_DIM), BF16)
    p_row = lax.broadcasted_iota(jnp.int32, (ATTN_TAIL, Q_PER_KV * ATTN_BLOCK), 0)
    def band_bias(n):
        kpos = pos0 + (n - 1) * ATTN_BLOCK + kj
        ok = band & (kpos >= 0) & (kpos < seq)
        bias = jnp.where(ok, 0.0, -jnp.inf)
        bias = jnp.concatenate([bias] * Q_PER_KV, axis=1)
        return bias[:ATTN_BLOCK], bias[2 * ATTN_BLOCK:]

    def scores(n, g):
        rows = slice(n * ATTN_BLOCK, (n + 1) * ATTN_BLOCK)
        kg = kall[n * ATTN_BLOCK:n * ATTN_BLOCK + span, g * HEAD_DIM:(g + 1) * HEAD_DIM]
        heads = [g * Q_PER_KV + r for r in range(Q_PER_KV)]
        qs = jnp.concatenate([q_ref[rows, hh * HEAD_DIM:(hh + 1) * HEAD_DIM] for hh in heads], axis=0)
        return lax.dot_general(kg, qs, (((1,), (1,)), ((), ())), preferred_element_type=F32)

    def finish(n, g, s, bias_prev, bias_next):
        rows = slice(n * ATTN_BLOCK, (n + 1) * ATTN_BLOCK)
        vg = vall[n * ATTN_BLOCK:n * ATTN_BLOCK + span, g * HEAD_DIM:(g + 1) * HEAD_DIM]
        v_aug = jnp.concatenate([jnp.concatenate([vg, ones_blk], axis=1), v_tail], axis=0)
        heads = [g * Q_PER_KV + r for r in range(Q_PER_KV)]
        blocks = [s[:ATTN_BLOCK] + bias_prev, s[ATTN_BLOCK:2 * ATTN_BLOCK], s[2 * ATTN_BLOCK:] + bias_next]
        sink = jnp.concatenate(
            [jnp.full((1, ATTN_BLOCK), sink_ref[layer, hh] * LOG2E, F32) for hh in heads], axis=1)
        mx = sink
        for sb in blocks:
            mx = jnp.maximum(mx, jnp.max(sb, axis=0, keepdims=True))
        es = jnp.exp2(sink - mx)
        es_hi = es.astype(BF16).astype(F32)
        p_tail = jnp.where(p_row == 0, es_hi, jnp.where(p_row == 1, es - es_hi, 0.0)).astype(BF16)
        p_aug = jnp.concatenate([jnp.exp2(sb - mx).astype(BF16) for sb in blocks] + [p_tail], axis=0)
        o = lax.dot_general(p_aug, v_aug, (((0,), (0,)), ((), ())), preferred_element_type=F32)
        o = o[:, :HEAD_DIM] / o[:, HEAD_DIM:]
        for r, hh in enumerate(heads):
            o_ref[rows, hh * HEAD_DIM:(hh + 1) * HEAD_DIM] = o[r * ATTN_BLOCK:(r + 1) * ATTN_BLOCK].astype(BF16)

    units = [(n, g) for n in range(tq // ATTN_BLOCK) for g in range(N_KV_HEADS)]
    biases = {}
    pending = [scores(*u) for u in units[:ATTN_LOOKAHEAD]]
    for k, (n, g) in enumerate(units):
        if k + ATTN_LOOKAHEAD < len(units):
            pending.append(scores(*units[k + ATTN_LOOKAHEAD]))
        if n not in biases:
            biases[n] = band_bias(n)
        finish(n, g, pending.pop(0), *biases[n])


def _attention(q, k, v, attn_sink, seq, layer):
    m = q.shape[0]
    tq = WIDE_TOKEN_TILE
    bpt = tq // ATTN_BLOCK
    last = m // ATTN_BLOCK - 1
    prev_map = lambda i: (jnp.maximum(i * bpt - 1, 0), 0)
    next_map = lambda i: (jnp.minimum(i * bpt + bpt, last), 0)
    cur_map = lambda i: (i, 0)
    return pl.pallas_call(
        functools.partial(_attn_kernel, layer=layer, seq=seq),
        grid=(m // tq,),
        in_specs=[
            pl.BlockSpec(memory_space=pltpu.SMEM),
            pl.BlockSpec((tq, Q_WIDTH), cur_map),
            pl.BlockSpec((ATTN_BLOCK, KV_WIDTH), prev_map),
            pl.BlockSpec((tq, KV_WIDTH), cur_map),
            pl.BlockSpec((ATTN_BLOCK, KV_WIDTH), next_map),
            pl.BlockSpec((ATTN_BLOCK, KV_WIDTH), prev_map),
            pl.BlockSpec((tq, KV_WIDTH), cur_map),
            pl.BlockSpec((ATTN_BLOCK, KV_WIDTH), next_map),
        ],
        out_specs=pl.BlockSpec((tq, Q_WIDTH), cur_map),
        out_shape=jax.ShapeDtypeStruct((m, Q_WIDTH), BF16),
        compiler_params=_params(("parallel",)),
        name=f"attn_l{layer}_m{m}",
    )(attn_sink, q, k, k, k, v, v, v)


def _merge_kernel(x_ref, mod_ref, ao_ref, so_ref, wgate_hbm, wba_hbm, wbs_hbm, wmo_hbm, o_ref,
                  h_ref, m_buf, wga_buf, wgb_buf, wba_buf, wbs_buf, wmo_buf, br_sem, mo_sem, *, layer, nj, tn):
    i, n_tiles = pl.program_id(0), pl.num_programs(0)
    t0 = i * nj
    gate = mod_ref[5:6, :]

    def br_copies(chunk, slot):
        col = pl.multiple_of(chunk * tn, tn)
        return (
            pltpu.make_async_copy(wgate_hbm.at[layer, :, pl.ds(col, tn)], wga_buf.at[slot], br_sem.at[slot, 0]),
            pltpu.make_async_copy(wgate_hbm.at[layer, :, pl.ds(D_MODEL + col, tn)], wgb_buf.at[slot],
                                  br_sem.at[slot, 1]),
            pltpu.make_async_copy(wba_hbm.at[layer, :, pl.ds(col, tn)], wba_buf.at[slot], br_sem.at[slot, 2]),
            pltpu.make_async_copy(wbs_hbm.at[layer, :, pl.ds(col, tn)], wbs_buf.at[slot], br_sem.at[slot, 3]),
        )

    def mo_copy(chunk, slot):
        row = pl.multiple_of(chunk * tn, tn)
        return pltpu.make_async_copy(wmo_hbm.at[layer, pl.ds(row, tn), :], wmo_buf.at[slot], mo_sem.at[slot])

    def merged_chunk(slot, rows=slice(None)):
        h = h_ref[rows, :]
        gate_a = _sigmoid(_dot(h, wga_buf[slot]))
        gate_b = _sigmoid(_dot(h, wgb_buf[slot]))
        attn = _dot(ao_ref[rows, :], wba_buf[slot])
        sgu = _dot(so_ref[rows, :], wbs_buf[slot])
        return (gate_a * attn + gate_b * sgu).astype(BF16)

    def boundary(j):
        return _stream_boundary(j, t0, nj, i + 1 < n_tiles, br_copies, mo_copy)

    @pl.when(i == 0)
    def _():
        _stream_prime(br_copies, mo_copy)

    slot0 = t0 % 2
    tm = x_ref.shape[0]
    shift, scale = mod_ref[3:4, :], mod_ref[4:5, :]
    for r in range(MERGE_ROW_PARTS):
        rows = slice(r * (tm // MERGE_ROW_PARTS), (r + 1) * (tm // MERGE_ROW_PARTS))
        x = x_ref[rows, :]
        h_ref[rows, :] = (x * (1.0 + scale) + shift).astype(BF16)
        o_ref[rows, :] = DEEPNORM_ALPHA * x
        m_buf[slot0, rows, :] = merged_chunk(slot0, rows)

    def trip(j, carry):
        slot = boundary(j)
        o_ref[...] += gate * _dot(m_buf[slot], wmo_buf[slot])
        m_buf[1 - slot] = merged_chunk(1 - slot)
        return carry

    lax.fori_loop(0, nj - 1, trip, 0)

    slot = boundary(nj - 1)
    o_ref[...] += gate * _dot(m_buf[slot], wmo_buf[slot])


def _merge_sublayer(x, mod, row0, seq, attn_o, sgu_o, w_gates, w_br_attn, w_br_sgu, w_mix_out, layer):
    m = x.shape[0]
    tm, tn = TOKEN_TILE, MERGE_TILE
    nj = D_MODEL // tn
    assert nj >= 2 and m % tm == 0 and seq % tm == 0
    return pl.pallas_call(
        functools.partial(_merge_kernel, layer=layer, nj=nj, tn=tn),
        grid=(m // tm,),
        in_specs=[
            pl.BlockSpec((tm, D_MODEL), lambda i: (i, 0)),
            pl.BlockSpec((None, None, N_MOD, D_MODEL), lambda i: (layer, row0 + (i * tm) // seq, 0, 0)),
            pl.BlockSpec((tm, Q_WIDTH), lambda i: (i, 0)),
            pl.BlockSpec((tm, SGU_WIDTH), lambda i: (i, 0)),
            pl.BlockSpec(memory_space=pl.ANY),
            pl.BlockSpec(memory_space=pl.ANY),
            pl.BlockSpec(memory_space=pl.ANY),
            pl.BlockSpec(memory_space=pl.ANY),
        ],
        out_specs=pl.BlockSpec((tm, D_MODEL), lambda i: (i, 0)),
        out_shape=jax.ShapeDtypeStruct((m, D_MODEL), F32),
        scratch_shapes=[
            pltpu.VMEM((tm, D_MODEL), BF16),
            pltpu.VMEM((2, tm, tn), BF16),
            pltpu.VMEM((2, D_MODEL, tn), BF16),
            pltpu.VMEM((2, D_MODEL, tn), BF16),
            pltpu.VMEM((2, Q_WIDTH, tn), BF16),
            pltpu.VMEM((2, SGU_WIDTH, tn), BF16),
            pltpu.VMEM((2, tn, D_MODEL), BF16),
            pltpu.SemaphoreType.DMA((2, 4)),
            pltpu.SemaphoreType.DMA((2,)),
        ],
        compiler_params=_params(("arbitrary",)),
        name=f"merge_l{layer}_m{m}",
    )(x, mod, attn_o, sgu_o, w_gates, w_br_attn, w_br_sgu, w_mix_out)


def _rope_tables(seq):
    half = HEAD_DIM // 2
    inv_freq = 1.0 / (ROPE_THETA ** (jnp.arange(half, dtype=F32) / half))
    ang = jnp.arange(seq, dtype=F32)[:, None] * inv_freq[None, :]
    cos = jnp.cos(ang)
    sin = jnp.sin(ang)
    return jnp.concatenate([cos, cos], axis=1), jnp.concatenate([-sin, sin], axis=1)


def kernel(x_prompt, x_sample, c_prompt, c_sample, w_ada, b_ada, ln_g, ln_b, ffn1_w_in, ffn1_w_out, w_mix_in, attn_sink, sgu_ln_g, sgu_ln_b, sgu_w, sgu_b, w_br_attn, w_br_sgu, w_mix_out, ffn2_w_in, ffn2_w_out):
    n_prompt = c_prompt.shape[0]
    c_all = jnp.concatenate([c_prompt, c_sample], axis=0)
    c_all = jnp.pad(c_all, ((0, -c_all.shape[0] % SUBLANE), (0, 0)))
    mod = _ada_mod(c_all, w_ada, b_ada)

    ffn1_in, ffn1_out = ffn1_w_in.astype(BF16), ffn1_w_out.astype(BF16)
    ffn2_in, ffn2_out = ffn2_w_in.astype(BF16), ffn2_w_out.astype(BF16)
    w_qkv = w_mix_in[:, :, :OFF_U].astype(BF16)
    w_usv = w_mix_in[:, :, OFF_U:OFF_GA].astype(BF16)
    w_gates = w_mix_in[:, :, OFF_GA:].astype(BF16)
    w_bra, w_brs, w_mo = w_br_attn.astype(BF16), w_br_sgu.astype(BF16), w_mix_out.astype(BF16)
    sgu_w16 = sgu_w.astype(BF16)
    sgu_b_bc = jnp.broadcast_to(sgu_b[..., None], sgu_b.shape + (LANE,))
    ln_g4 = ln_g.reshape(DEPTH, 3, 1, D_MODEL)
    ln_b4 = ln_b.reshape(DEPTH, 3, 1, D_MODEL)
    sgu_g3 = sgu_ln_g.reshape(DEPTH, 1, SGU_WIDTH)
    sgu_b3 = sgu_ln_b.reshape(DEPTH, 1, SGU_WIDTH)

    def run(x3, row0):
        bsz, seq, _ = x3.shape
        x = x3.reshape(bsz * seq, D_MODEL)
        cos_t, sin_t = _rope_tables(seq)
        for l in range(DEPTH):
            x = _ffn_sublayer(x, mod, row0, seq, ffn1_in, ffn1_out, ln_g4, ln_b4, l, 0,
                              ln_in=(l - 1, 2) if l > 0 else None, ln_out=True)
            q, k, v = _qkv_proj(x, mod, row0, seq, w_qkv, cos_t, sin_t, l)
            sgu_o = _sgu_branch(x, mod, row0, seq, w_usv, sgu_g3, sgu_b3, sgu_w16, sgu_b_bc, l)
            attn_o = _attention(q, k, v, attn_sink, seq, l)
            x = _merge_sublayer(x, mod, row0, seq, attn_o, sgu_o, w_gates, w_bra, w_brs, w_mo, l)
            x = _ffn_sublayer(x, mod, row0, seq, ffn2_in, ffn2_out, ln_g4, ln_b4, l, 2,
                              ln_in=(l, 1), ln_out=l == DEPTH - 1)
        return x.reshape(bsz, seq, D_MODEL)

    return run(x_prompt, 0), run(x_sample, n_prompt)
```
